```python
import jax, jax.numpy as jnp
from jax import lax
import numpy as np

D_MODEL = 1024
BATCH = 4
SEQ = 8192
DEPTH = 2

D_FF = 2816
FFN_RESIDUAL = 0.5
SSD_HEADS = 16
SSD_HEAD_DIM = 64
D_SSM = SSD_HEADS * SSD_HEAD_DIM
SSD_GROUPS = 2
D_STATE = 128
CONV_WIDTH = 4
CHUNK = 128
CONV_CH = D_SSM + 2 * SSD_GROUPS * D_STATE
POOL_WINDOWS = (2, 4, 8, 16)
POOL_GROUP = 256
D_POOL = POOL_GROUP * len(POOL_WINDOWS)
IN_PROJ = D_SSM + CONV_CH + SSD_HEADS + D_POOL
D_MIX_EVEN = D_SSM + D_POOL
ATTN_HEADS = 16
ATTN_KV_HEADS = 4
ATTN_GROUP = ATTN_HEADS // ATTN_KV_HEADS
ATTN_HEAD_DIM = 64
WINDOW = 128
BLOCK = 128
QKV_DIM = (ATTN_HEADS + 2 * ATTN_KV_HEADS) * ATTN_HEAD_DIM
MEM_LEN = 256
MEM_HEADS = 4
MEM_HEAD_DIM = D_MODEL // MEM_HEADS
EPS = 1e-6
N_EVEN = (DEPTH + 1) // 2
N_ODD = DEPTH // 2

kernel_name = "hybrid_ssd_pool_swa_macaron_trunk"


def rms_norm(x, gain):
    xf = x.astype(jnp.float32)
    y = xf * lax.rsqrt(jnp.mean(xf * xf, axis=-1, keepdims=True) + EPS)
    return (y * gain.astype(jnp.float32)).astype(x.dtype)


def swiglu(h, wi, wo):
    gate, up = jnp.split(h @ wi, 2, axis=-1)
    return (jax.nn.silu(gate) * up) @ wo


def alibi_slopes(n):
    return jnp.asarray(2.0 ** (-8.0 * (np.arange(n) + 1) / n), dtype=jnp.float32)


def causal_depthwise_conv(u, w, b):
    ch = u.shape[-1]
    y = lax.conv_general_dilated(u, w[:, None, :].astype(u.dtype), window_strides=(1,),
                                 padding=((CONV_WIDTH - 1, 0),),
                                 dimension_numbers=('NWC', 'WIO', 'NWC'),
                                 feature_group_count=ch)
    return y + b


def ssd_chunked(x, dt, a, b_in, c_in):
    bsz, T = x.shape[0], x.shape[1]
    nc = T // CHUNK
    J = SSD_HEADS // SSD_GROUPS
    xr = (x * dt[..., None]).reshape(bsz, nc, CHUNK, SSD_GROUPS, J, SSD_HEAD_DIM)
    adt = (dt * a).reshape(bsz, nc, CHUNK, SSD_GROUPS, J).transpose(0, 1, 3, 4, 2)
    acs = jnp.cumsum(adt, axis=-1)
    br = b_in.reshape(bsz, nc, CHUNK, SSD_GROUPS, D_STATE)
    cr = c_in.reshape(bsz, nc, CHUNK, SSD_GROUPS, D_STATE)
    causal = np.tril(np.ones((CHUNK, CHUNK), dtype=bool))
    seg = jnp.exp(jnp.where(causal, acs[..., :, None] - acs[..., None, :], -jnp.inf))
    cb = jnp.einsum('bclgn,bcsgn->bcgls', cr, br)
    y_diag = jnp.einsum('bcgjls,bcsgjp->bclgjp', cb[:, :, :, None] * seg, xr)
    decay_to_end = jnp.exp(acs[..., -1:] - acs).transpose(0, 1, 4, 2, 3)
    states = jnp.einsum('bclgn,bclgjp->bcgjpn', br, xr * decay_to_end[..., None])
    chunk_decay = jnp.exp(acs[..., -1])

    def carry_state(h, inp):
        st, dec = inp
        return h * dec[..., None, None] + st, h

    _, states_in = lax.scan(carry_state, jnp.zeros_like(states[:, 0]),
                            (jnp.moveaxis(states, 1, 0), jnp.moveaxis(chunk_decay, 1, 0)))
    states_in = jnp.moveaxis(states_in, 0, 1)
    decay_from_start = jnp.exp(acs).transpose(0, 1, 4, 2, 3)
    y_off = jnp.einsum('bclgn,bcgjpn->bclgjp', cr, states_in) * decay_from_start[..., None]
    return (y_diag + y_off).reshape(bsz, T, SSD_HEADS, SSD_HEAD_DIM)


def multiscale_causal_pool(u, pool_w, pool_scale):
    T = u.shape[1]
    t = jnp.arange(T)
    outs = []
    for k, w in enumerate(POOL_WINDOWS):
        ug = u[..., k * POOL_GROUP:(k + 1) * POOL_GROUP].astype(jnp.float32)
        csum = jnp.cumsum(ug, axis=1)
        csum_shift = jnp.pad(csum, ((0, 0), (w, 0), (0, 0)))[:, :T]
        count = jnp.minimum(t + 1, w).astype(jnp.float32)[None, :, None]
        pooled = ((csum - csum_shift) / count - ug).astype(u.dtype)
        outs.append(pooled @ pool_w[k])
    return jnp.concatenate(outs, axis=-1) * pool_scale


def ssd_pool_mixer(h, in_proj, conv_w, conv_b, dt_bias, a_log, d_skip, ssd_norm,
                   pool_w, pool_scale, out_proj):
    bsz, T, _ = h.shape
    proj = h @ in_proj
    z, xbc, dt_raw, u_pool = jnp.split(
        proj, [D_SSM, D_SSM + CONV_CH, D_SSM + CONV_CH + SSD_HEADS], axis=-1)
    xbc = jax.nn.silu(causal_depthwise_conv(xbc, conv_w, conv_b))
    xs, b_in, c_in = jnp.split(xbc, [D_SSM, D_SSM + SSD_GROUPS * D_STATE], axis=-1)
    dt = jax.nn.softplus(dt_raw.astype(jnp.float32) + dt_bias.astype(jnp.float32))
    a = -jnp.exp(a_log.astype(jnp.float32))
    xs_h = xs.reshape(bsz, T, SSD_HEADS, SSD_HEAD_DIM).astype(jnp.float32)
    y = ssd_chunked(xs_h, dt, a,
                    b_in.reshape(bsz, T, SSD_GROUPS, D_STATE).astype(jnp.float32),
                    c_in.reshape(bsz, T, SSD_GROUPS, D_STATE).astype(jnp.float32))
    y = y + d_skip.astype(jnp.float32)[:, None] * xs_h
    y = y.reshape(bsz, T, D_SSM) * jax.nn.silu(z.astype(jnp.float32))
    y = rms_norm(y.reshape(bsz, T, SSD_GROUPS, D_SSM // SSD_GROUPS),
                 ssd_norm.reshape(SSD_GROUPS, D_SSM // SSD_GROUPS))
    y_ssd = y.reshape(bsz, T, D_SSM).astype(h.dtype)
    y_pool = multiscale_causal_pool(u_pool, pool_w, pool_scale)
    return jnp.concatenate([y_ssd, y_pool], axis=-1) @ out_proj


def swa_sink_attention(h, wqkv, bqkv, qnorm, knorm, sinks, wo, bo):
    bsz, T, _ = h.shape
    nb = T // BLOCK
    HD, KVH, G = ATTN_HEAD_DIM, ATTN_KV_HEADS, ATTN_GROUP
    qkv = h @ wqkv + bqkv
    q, k, v = jnp.split(qkv, [ATTN_HEADS * HD, (ATTN_HEADS + KVH) * HD], axis=-1)
    q = rms_norm(q.reshape(bsz, T, KVH, G, HD), qnorm)
    k = rms_norm(k.reshape(bsz, T, KVH, HD), knorm)
    v = v.reshape(bsz, T, KVH, HD)
    qb = q.reshape(bsz, nb, BLOCK, KVH, G, HD)

    def band(t):
        tb = t.reshape(bsz, nb, BLOCK, KVH, HD)
        prev = jnp.pad(tb, ((0, 0), (1, 0), (0, 0), (0, 0), (0, 0)))[:, :nb]
        return jnp.concatenate([prev, tb], axis=2)

    kk, vv = band(k), band(v)
    s = jnp.einsum('bnqhgd,bnkhd->bnhgqk', qb, kk).astype(jnp.float32) * (HD ** -0.5)
    dist = np.arange(BLOCK)[:, None] + BLOCK - np.arange(2 * BLOCK)[None, :]
    in_window = (dist >= 0) & (dist < WINDOW)
    has_prev = (jnp.arange(nb)[:, None, None] > 0) | (np.arange(2 * BLOCK) >= BLOCK)[None, None, :]
    mask = in_window[None] & has_prev
    slopes = alibi_slopes(ATTN_HEADS).reshape(KVH, G)
    s = s - slopes[:, :, None, None] * jnp.asarray(dist, dtype=jnp.float32)
    s = jnp.where(mask[None, :, None, None], s, -jnp.inf)
    sink = jnp.broadcast_to(sinks.astype(jnp.float32).reshape(KVH, G, 1, 1), s.shape[:-1] + (1,))
    p = jax.nn.softmax(jnp.concatenate([s, sink], axis=-1), axis=-1)[..., :-1]
    o = jnp.einsum('bnhgqk,bnkhd->bnqhgd', p.astype(h.dtype), vv)
    return o.reshape(bsz, T, ATTN_HEADS * HD) @ wo + bo


def memory_cross_attention(h, mem_k, mem_v, wq, qnorm, wo):
    bsz, T, _ = h.shape
    q = rms_norm((h @ wq).reshape(bsz, T, MEM_HEADS, MEM_HEAD_DIM), qnorm)
    s = jnp.einsum('bthd,bmhd->bhtm', q, mem_k).astype(jnp.float32) * (MEM_HEAD_DIM ** -0.5)
    p = jax.nn.softmax(s, axis=-1).astype(h.dtype)
    o = jnp.einsum('bhtm,bmhd->bthd', p, mem_v)
    return o.reshape(bsz, T, D_MODEL) @ wo


def setup_inputs(seed: int = 0) -> dict:
    key = jax.random.key(seed)
    ks = iter(jax.random.split(key, 48))
    f32 = jnp.float32

    def nrm(shape, scale):
        return jax.random.normal(next(ks), shape, f32) * scale

    def gain(shape):
        return 1.0 + nrm(shape, 0.05)

    L = DEPTH
    dt0 = jnp.exp(jax.random.uniform(next(ks), (N_EVEN, SSD_HEADS), f32,
                                     np.log(1e-3), np.log(1e-1)))
    dt_bias = dt0 + jnp.log(-jnp.expm1(-dt0))
    a_log = jnp.log(jax.random.uniform(next(ks), (N_EVEN, SSD_HEADS), f32, 1.0, 16.0))
    return {
        "x": nrm((BATCH, SEQ, D_MODEL), 1.0),
        "mem": nrm((BATCH, MEM_LEN, D_MODEL), 1.0),
        "mem_norm": gain((D_MODEL,)),
        "mem_wkv": nrm((D_MODEL, 2 * D_MODEL), D_MODEL ** -0.5),
        "mem_knorm": gain((MEM_HEAD_DIM,)),
        "ffn1_norm": gain((L, D_MODEL)),
        "ffn1_wi": nrm((L, D_MODEL, 2 * D_FF), D_MODEL ** -0.5),
        "ffn1_wo": nrm((L, D_FF, D_MODEL), D_FF ** -0.5),
        "mix_norm": gain((L, D_MODEL)),
        "ssd_in_proj": nrm((N_EVEN, D_MODEL, IN_PROJ), D_MODEL ** -0.5),
        "ssd_conv_w": nrm((N_EVEN, CONV_WIDTH, CONV_CH), CONV_WIDTH ** -0.5),
        "ssd_conv_b": nrm((N_EVEN, CONV_CH), 0.02),
        "ssd_dt_bias": dt_bias,
        "ssd_a_log": a_log,
        "ssd_d": 1.0 + nrm((N_EVEN, SSD_HEADS), 0.1),
        "ssd_norm": gain((N_EVEN, D_SSM)),
        "pool_w": nrm((N_EVEN, len(POOL_WINDOWS), POOL_GROUP, POOL_GROUP), POOL_GROUP ** -0.5),
        "pool_scale": gain((N_EVEN, D_POOL)),
        "even_out_proj": nrm((N_EVEN, D_MIX_EVEN, D_MODEL), D_MIX_EVEN ** -0.5),
        "attn_wqkv": nrm((N_ODD, D_MODEL, QKV_DIM), D_MODEL ** -0.5),
        "attn_bqkv": nrm((N_ODD, QKV_DIM), 0.02),
        "attn_qnorm": gain((N_ODD, ATTN_HEAD_DIM)),
        "attn_knorm": gain((N_ODD, ATTN_HEAD_DIM)),
        "attn_sinks": nrm((N_ODD, ATTN_HEADS), 0.5),
        "attn_wo": nrm((N_ODD, ATTN_HEADS * ATTN_HEAD_DIM, D_MODEL), (ATTN_HEADS * ATTN_HEAD_DIM) ** -0.5),
        "attn_bo": nrm((N_ODD, D_MODEL), 0.02),
        "xattn_norm": gain((L, D_MODEL)),
        "xattn_wq": nrm((L, D_MODEL, D_MODEL), D_MODEL ** -0.5),
        "xattn_qnorm": gain((L, MEM_HEAD_DIM)),
        "xattn_wo": nrm((L, D_MODEL, D_MODEL), D_MODEL ** -0.5),
        "ffn2_norm": gain((L, D_MODEL)),
        "ffn2_wi": nrm((L, D_MODEL, 2 * D_FF), D_MODEL ** -0.5),
        "ffn2_wo": nrm((L, D_FF, D_MODEL), D_FF ** -0.5),
    }


def reference(x, mem, mem_norm, mem_wkv, mem_knorm, ffn1_norm, ffn1_wi, ffn1_wo, mix_norm,
              ssd_in_proj, ssd_conv_w, ssd_conv_b, ssd_dt_bias, ssd_a_log, ssd_d, ssd_norm,
              pool_w, pool_scale, even_out_proj, attn_wqkv, attn_bqkv, attn_qnorm, attn_knorm,
              attn_sinks, attn_wo, attn_bo, xattn_norm, xattn_wq, xattn_qnorm, xattn_wo,
              ffn2_norm, ffn2_wi, ffn2_wo):
    bsz = mem.shape[0]
    mem_kv = rms_norm(mem, mem_norm) @ mem_wkv
    mem_k, mem_v = jnp.split(mem_kv, 2, axis=-1)
    mem_k = rms_norm(mem_k.reshape(bsz, MEM_LEN, MEM_HEADS, MEM_HEAD_DIM), mem_knorm)
    mem_v = mem_v.reshape(bsz, MEM_LEN, MEM_HEADS, MEM_HEAD_DIM)
    for i in range(DEPTH):
        x = x + FFN_RESIDUAL * swiglu(rms_norm(x, ffn1_norm[i]), ffn1_wi[i], ffn1_wo[i])
        h = rms_norm(x, mix_norm[i])
        if i % 2 == 0:
            e = i // 2
            x = x + ssd_pool_mixer(h, ssd_in_proj[e], ssd_conv_w[e], ssd_conv_b[e], ssd_dt_bias[e],
                                   ssd_a_log[e], ssd_d[e], ssd_norm[e], pool_w[e], pool_scale[e],
                                   even_out_proj[e])
        else:
            o = i // 2
            x = x + swa_sink_attention(h, attn_wqkv[o], attn_bqkv[o], attn_qnorm[o], attn_knorm[o],
                                       attn_sinks[o], attn_wo[o], attn_bo[o])
        x = x + memory_cross_attention(rms_norm(x, xattn_norm[i]), mem_k, mem_v,
                                       xattn_wq[i], xattn_qnorm[i], xattn_wo[i])
        x = x + FFN_RESIDUAL * swiglu(rms_norm(x, ffn2_norm[i]), ffn2_wi[i], ffn2_wo[i])
    return x
```

```python
import functools

import numpy as np
import jax
import jax.numpy as jnp
from jax import lax
from jax.experimental import pallas as pl
from jax.experimental.pallas import tpu as pltpu

F32 = jnp.float32
BF16 = jnp.bfloat16

EPS = 1e-6
FFN_RESIDUAL = 0.5
SSD_HEADS = 16
SSD_HEAD_DIM = 64
SSD_GROUPS = 2
D_STATE = 128
CONV_WIDTH = 4
CHUNK = 128
D_SSM = SSD_HEADS * SSD_HEAD_DIM
GROUP_W = D_SSM // SSD_GROUPS
CONV_CH = D_SSM + 2 * SSD_GROUPS * D_STATE
POOL_WINDOWS = (2, 4, 8, 16)
POOL_GROUP = 256
D_POOL = POOL_GROUP * len(POOL_WINDOWS)
POOL_HALO = 16
CONV_HALO = 8
ATTN_HEADS = 16
ATTN_KV_HEADS = 4
ATTN_GROUP = ATTN_HEADS // ATTN_KV_HEADS
ATTN_HEAD_DIM = 64
WINDOW = 128
BLOCK = 128
MEM_HEADS = 4

LANES = 128
HALF = 64

TM_FFN = 512
TM_XATTN = 512
TM_EVEN = 256
TM_ODD = 512
VMEM_LIMIT_BYTES = 56 * 1024 * 1024
FFN_CHUNKS = (1024, 1024, 768)


def _rms(x, gain):
    ms = jnp.mean(x * x, axis=-1, keepdims=True)
    return x * lax.rsqrt(ms + EPS) * gain


def _silu(x):
    return x * (1.0 / (1.0 + jnp.exp(-x)))


def _split2(x):
    hi = x.astype(BF16)
    lo = (x - hi.astype(F32)).astype(BF16)
    return hi, lo


def _resident(shape):
    nd = len(shape)
    return pl.BlockSpec(shape, lambda *_: (0,) * nd, pipeline_mode=pl.Buffered(1))


def _params(n_axes):
    return pltpu.CompilerParams(
        dimension_semantics=("arbitrary",) * n_axes,
        vmem_limit_bytes=VMEM_LIMIT_BYTES,
    )


def _memkv_kernel(mem_ref, g_ref, wkv_ref, kn_ref, kt_ref, v_ref):
    d = mem_ref.shape[-1]
    hd = d // MEM_HEADS
    h = _rms(mem_ref[0], g_ref[...]).astype(BF16)
    kv = jnp.dot(h, wkv_ref[...], preferred_element_type=F32)
    for i in range(MEM_HEADS):
        kh = _rms(kv[:, i * hd:(i + 1) * hd], kn_ref[...])
        kt_ref[0, i] = kh.T.astype(BF16)
    v_ref[0] = kv[:, d:].astype(BF16)


def _mem_kv(mem, mem_norm, wkv, mem_knorm):
    b, m, d = mem.shape
    hd = d // MEM_HEADS
    return pl.pallas_call(
        _memkv_kernel,
        grid=(b,),
        in_specs=[
            pl.BlockSpec((1, m, d), lambda i: (i, 0, 0)),
            _resident((1, d)),
            _resident((d, 2 * d)),
            _resident((1, hd)),
        ],
        out_specs=[
            pl.BlockSpec((1, MEM_HEADS, hd, m), lambda i: (i, 0, 0, 0)),
            pl.BlockSpec((1, m, d), lambda i: (i, 0, 0)),
        ],
        out_shape=[
            jax.ShapeDtypeStruct((b, MEM_HEADS, hd, m), BF16),
            jax.ShapeDtypeStruct((b, m, d), BF16),
        ],
        compiler_params=_params(1),
        name="mem_kv",
    )(mem, mem_norm.reshape(1, d), wkv.astype(BF16), mem_knorm.reshape(1, hd))


def _ffn_kernel(x_ref, g_ref, wi_ref, wo_ref, o_ref, act_ref):
    d_ff = wo_ref.shape[0]
    x = x_ref[...]
    h = _rms(x, g_ref[...]).astype(BF16)
    lo = 0
    for ck in FFN_CHUNKS:
        gate = jnp.dot(h, wi_ref[:, lo:lo + ck], preferred_element_type=F32)
        up = jnp.dot(h, wi_ref[:, d_ff + lo:d_ff + lo + ck], preferred_element_type=F32)
        act_ref[:, lo:lo + ck] = (_silu(gate) * up).astype(BF16)
        lo += ck
    y = jnp.dot(act_ref[...], wo_ref[...], preferred_element_type=F32)
    o_ref[...] = x + FFN_RESIDUAL * y


def _ffn(x2, gain, wi, wo):
    n, d = x2.shape
    d_ff = wo.shape[0]
    assert sum(FFN_CHUNKS) == d_ff and n % TM_FFN == 0
    return pl.pallas_call(
        _ffn_kernel,
        grid=(n // TM_FFN,),
        in_specs=[
            pl.BlockSpec((TM_FFN, d), lambda i: (i, 0)),
            _resident((1, d)),
            _resident((d, 2 * d_ff)),
            _resident((d_ff, d)),
        ],
        out_specs=pl.BlockSpec((TM_FFN, d), lambda i: (i, 0)),
        out_shape=jax.ShapeDtypeStruct((n, d), F32),
        scratch_shapes=[pltpu.VMEM((TM_FFN, d_ff), BF16)],
        compiler_params=_params(1),
        name="ffn",
    )(x2, gain.reshape(1, d), wi.astype(BF16), wo.astype(BF16))


def _xattn_kernel(x_ref, g_ref, wq_ref, qn_ref, kt_ref, v_ref, wo_ref, o_ref, att_ref):
    d = x_ref.shape[-1]
    hd = d // MEM_HEADS
    x = x_ref[0]
    h = _rms(x, g_ref[...]).astype(BF16)
    q = jnp.dot(h, wq_ref[...], preferred_element_type=F32)
    for i in range(MEM_HEADS):
        cols = slice(i * hd, (i + 1) * hd)
        qh = (_rms(q[:, cols], qn_ref[...]) * (hd ** -0.5)).astype(BF16)
        s = jnp.dot(qh, kt_ref[0, i], preferred_element_type=F32)
        p = jnp.exp(s - jnp.max(s, axis=-1, keepdims=True))
        inv = 1.0 / jnp.sum(p, axis=-1, keepdims=True)
        o = jnp.dot(p.astype(BF16), v_ref[0, :, cols], preferred_element_type=F32)
        att_ref[:, cols] = (o * inv).astype(BF16)
    y = jnp.dot(att_ref[...], wo_ref[...], preferred_element_type=F32)
    o_ref[0] = x + y


def _xattn(x3, gain, wq, qnorm, mem_kt, mem_v, wo):
    b, t, d = x3.shape
    hd = d // MEM_HEADS
    m = mem_v.shape[1]
    tm = TM_XATTN
    assert t % tm == 0
    return pl.pallas_call(
        _xattn_kernel,
        grid=(b, t // tm),
        in_specs=[
            pl.BlockSpec((1, tm, d), lambda i, j: (i, j, 0)),
            _resident((1, d)),
            _resident((d, d)),
            _resident((1, hd)),
            pl.BlockSpec((1, MEM_HEADS, hd, m), lambda i, j: (i, 0, 0, 0)),
            pl.BlockSpec((1, m, d), lambda i, j: (i, 0, 0)),
            _resident((d, d)),
        ],
        out_specs=pl.BlockSpec((1, tm, d), lambda i, j: (i, j, 0)),
        out_shape=jax.ShapeDtypeStruct((b, t, d), F32),
        scratch_shapes=[pltpu.VMEM((tm, d), BF16)],
        compiler_params=_params(2),
        name="xattn",
    )(x3, gain.reshape(1, d), wq.astype(BF16), qnorm.reshape(1, hd), mem_kt, mem_v,
      wo.astype(BF16))


def _even_kernel(x_ref, g_ref, win_ref, cw_ref, cb_ref, dtb_ref, alog_ref, dskip_ref, norm_ref,
                 pw_ref, ps_ref, wout_ref, exph_ref, tril_ref, o_ref,
                 conv_scr, act_scr, dt_scr, y_scr, pool_scr, mix_scr, state_scr):
    t = pl.program_id(1)
    tm = x_ref.shape[1]
    n_chunks = tm // CHUNK

    @pl.when(t == 0)
    def _():
        conv_scr[0:CONV_HALO, :] = jnp.zeros((CONV_HALO, CONV_CH), F32)
        pool_scr[0:POOL_HALO, :] = jnp.zeros((POOL_HALO, D_POOL), F32)
        state_scr[...] = jnp.zeros(state_scr.shape, F32)

    x = x_ref[0]
    h = _rms(x, g_ref[...]).astype(BF16)
    proj = jnp.dot(h, win_ref[...], preferred_element_type=F32)
    z = proj[:, 0:D_SSM]
    o1 = D_SSM + CONV_CH
    o2 = o1 + D_POOL
    conv_scr[CONV_HALO:CONV_HALO + tm, :] = proj[:, D_SSM:o1]
    pool_scr[POOL_HALO:POOL_HALO + tm, :] = proj[:, o1:o2]
    dt_raw = proj[:, o2:o2 + LANES] + dtb_ref[...]
    dt_scr[...] = jnp.maximum(dt_raw, 0.0) + jnp.log1p(jnp.exp(-jnp.abs(dt_raw)))

    acc = cb_ref[...] + conv_scr[pl.ds(CONV_HALO - CONV_WIDTH + 1, tm), :] * cw_ref[0:1, :]
    for k in range(1, CONV_WIDTH):
        acc = acc + conv_scr[pl.ds(CONV_HALO - CONV_WIDTH + 1 + k, tm), :] * cw_ref[k:k + 1, :]
    act_scr[...] = _silu(acc)
    conv_scr[0:CONV_HALO, :] = conv_scr[tm:tm + CONV_HALO, :]

    a_row = -jnp.exp(alog_ref[...])
    lane = lax.broadcasted_iota(jnp.int32, (CHUNK, LANES), 1)
    low_half = lane < HALF
    causal = (lax.broadcasted_iota(jnp.int32, (CHUNK, CHUNK), 0)
              >= lax.broadcasted_iota(jnp.int32, (CHUNK, CHUNK), 1))

    def chunk_body(c, carry):
        r0 = pl.multiple_of(c * CHUNK, CHUNK)
        rows = pl.ds(r0, CHUNK)
        xs = act_scr[rows, 0:D_SSM]
        dt = dt_scr[rows, :]
        adt = dt * a_row
        hi = adt.astype(BF16)
        r1 = adt - hi.astype(F32)
        mid = r1.astype(BF16)
        lo = (r1 - mid.astype(F32)).astype(BF16)
        acs = jnp.dot(tril_ref[...], jnp.concatenate([hi, mid, lo], axis=0),
                      preferred_element_type=F32)
        acs_last = acs[CHUNK - 1:CHUNK, :]
        stacked = jnp.concatenate([dt, jnp.exp(acs_last - acs), jnp.exp(acs)], axis=0)
        s_hi, s_lo = _split2(stacked)
        expanded = jnp.dot(jnp.concatenate([s_hi, s_lo], axis=1), exph_ref[...],
                           preferred_element_type=F32)
        dt_exp = expanded[0:CHUNK]
        dte_exp = expanded[CHUNK:2 * CHUNK]
        dfs_exp = expanded[2 * CHUNK:3 * CHUNK]
        xdt = xs * dt_exp
        xdt_b = xdt.astype(BF16)
        xw_b = (xdt * dte_exp).astype(BF16)
        acs_row = acs.T

        for g in range(SSD_GROUPS):
            gcols = slice(g * GROUP_W, (g + 1) * GROUP_W)
            b_g = act_scr[rows, D_SSM + g * D_STATE:D_SSM + (g + 1) * D_STATE]
            c_g = act_scr[rows, D_SSM + SSD_GROUPS * D_STATE + g * D_STATE:
                          D_SSM + SSD_GROUPS * D_STATE + (g + 1) * D_STATE].astype(BF16)
            bt_g = b_g.T.astype(BF16)
            cb = jnp.dot(c_g, bt_g, preferred_element_type=F32)
            s_new = jnp.dot(bt_g, xw_b[:, gcols], preferred_element_type=F32)
            s_in = state_scr[g]
            y_off = jnp.dot(c_g, s_in.astype(BF16), preferred_element_type=F32) * dfs_exp[:, gcols]
            state_scr[g] = s_in * dfs_exp[CHUNK - 1:CHUNK, gcols] + s_new
            heads_per_group = SSD_HEADS // SSD_GROUPS
            for pair in range(heads_per_group // 2):
                y2 = None
                pcols = slice(g * GROUP_W + pair * LANES, g * GROUP_W + (pair + 1) * LANES)
                x2 = xdt_b[:, pcols]
                for parity in range(2):
                    hh = g * heads_per_group + 2 * pair + parity
                    seg = jnp.where(causal, jnp.exp(acs[:, hh:hh + 1] - acs_row[hh:hh + 1, :]), 0.0)
                    w = (cb * seg).astype(BF16)
                    keep = low_half if parity == 0 else jnp.logical_not(low_half)
                    xh = jnp.where(keep, x2, jnp.zeros_like(x2))
                    part = jnp.dot(w, xh, preferred_element_type=F32)
                    y2 = part if y2 is None else y2 + part
                y_scr[rows, pcols] = y2 + y_off[:, pair * LANES:(pair + 1) * LANES]
        return carry

    lax.fori_loop(0, n_chunks, chunk_body, 0)

    y = (y_scr[...] + dskip_ref[...] * act_scr[:, 0:D_SSM]) * _silu(z)
    for g in range(SSD_GROUPS):
        gcols = slice(g * GROUP_W, (g + 1) * GROUP_W)
        mix_scr[:, gcols] = _rms(y[:, gcols], norm_ref[:, gcols]).astype(BF16)

    pos = t * tm + lax.broadcasted_iota(jnp.int32, (tm, 1), 0)
    for k, w in enumerate(POOL_WINDOWS):
        kcols = slice(k * POOL_GROUP, (k + 1) * POOL_GROUP)
        s = pool_scr[:, kcols]
        shift = 1
        while shift < w:
            s = s + pltpu.roll(s, shift, axis=0)
            shift *= 2
        inv_count = 1.0 / jnp.minimum(pos + 1, w).astype(F32)
        pooled = s[POOL_HALO:, :] * inv_count - pool_scr[POOL_HALO:POOL_HALO + tm, kcols]
        yk = jnp.dot(pooled.astype(BF16), pw_ref[k], preferred_element_type=F32)
        mix_scr[:, D_SSM + k * POOL_GROUP:D_SSM + (k + 1) * POOL_GROUP] = (
            yk * ps_ref[:, kcols]).astype(BF16)
    pool_scr[0:POOL_HALO, :] = pool_scr[tm:tm + POOL_HALO, :]

    o_ref[0] = x + jnp.dot(mix_scr[...], wout_ref[...], preferred_element_type=F32)


def _head_expand_matrix():
    m = np.zeros((2 * LANES, D_SSM), np.float32)
    for hh in range(SSD_HEADS):
        m[hh, hh * SSD_HEAD_DIM:(hh + 1) * SSD_HEAD_DIM] = 1.0
        m[LANES + hh, hh * SSD_HEAD_DIM:(hh + 1) * SSD_HEAD_DIM] = 1.0
    return jnp.asarray(m, BF16)


def _even_mixer(x3, gain, in_proj, conv_w, conv_b, dt_bias, a_log, d_skip, ssd_norm,
                pool_w, pool_scale, out_proj):
    b, t, d = x3.shape
    tm = TM_EVEN
    assert t % tm == 0 and tm % CHUNK == 0
    o_xbc = D_SSM
    o_dt = D_SSM + CONV_CH
    o_pool = o_dt + SSD_HEADS
    win = jnp.concatenate(
        [in_proj[:, :o_dt], in_proj[:, o_pool:], in_proj[:, o_dt:o_pool],
         jnp.zeros((d, LANES - SSD_HEADS), in_proj.dtype)], axis=1).astype(BF16)
    n_in = win.shape[1]
    pad = LANES - SSD_HEADS
    dtb = jnp.pad(dt_bias, (0, pad)).reshape(1, LANES)
    alog = jnp.pad(a_log, (0, pad)).reshape(1, LANES)
    dskip = jnp.repeat(d_skip, SSD_HEAD_DIM).reshape(1, D_SSM)
    tril3 = jnp.asarray(np.tile(np.tril(np.ones((CHUNK, CHUNK), np.float32)), (1, 3)), BF16)
    n_g = len(POOL_WINDOWS)
    return pl.pallas_call(
        _even_kernel,
        grid=(b, t // tm),
        in_specs=[
            pl.BlockSpec((1, tm, d), lambda i, j: (i, j, 0)),
            _resident((1, d)),
            _resident((d, n_in)),
            _resident((CONV_WIDTH, CONV_CH)),
            _resident((1, CONV_CH)),
            _resident((1, LANES)),
            _resident((1, LANES)),
            _resident((1, D_SSM)),
            _resident((1, D_SSM)),
            _resident((n_g, POOL_GROUP, POOL_GROUP)),
            _resident((1, D_POOL)),
            _resident((D_SSM + D_POOL, d)),
            _resident((2 * LANES, D_SSM)),
            _resident((CHUNK, 3 * CHUNK)),
        ],
        out_specs=pl.BlockSpec((1, tm, d), lambda i, j: (i, j, 0)),
        out_shape=jax.ShapeDtypeStruct((b, t, d), F32),
        scratch_shapes=[
            pltpu.VMEM((tm + CONV_HALO, CONV_CH), F32),
            pltpu.VMEM((tm, CONV_CH), F32),
            pltpu.VMEM((tm, LANES), F32),
            pltpu.VMEM((tm, D_SSM), F32),
            pltpu.VMEM((tm + POOL_HALO, D_POOL), F32),
            pltpu.VMEM((tm, D_SSM + D_POOL), BF16),
            pltpu.VMEM((SSD_GROUPS, D_STATE, GROUP_W), F32),
        ],
        compiler_params=_params(2),
        name="even_mixer",
    )(x3, gain.reshape(1, d), win, conv_w, conv_b.reshape(1, CONV_CH), dtb, alog, dskip,
      ssd_norm.reshape(1, D_SSM), pool_w.astype(BF16), pool_scale.reshape(1, D_POOL),
      out_proj.astype(BF16), _head_expand_matrix(), tril3)


N_QK_HEADS = ATTN_HEADS + ATTN_KV_HEADS
QK_W = N_QK_HEADS * ATTN_HEAD_DIM
Q_W = ATTN_HEADS * ATTN_HEAD_DIM
KV_W = ATTN_KV_HEADS * ATTN_HEAD_DIM


def _odd_kernel(sink_ref, x_ref, g_ref, wqkv_ref, bqkv_ref, gqk_ref, seg_ref, exp_ref, bias_ref,
                wo_ref, bo_ref, o_ref, q_scr, k_scr, v_scr, att_scr):
    t = pl.program_id(1)
    tm = x_ref.shape[1]
    n_blocks = tm // BLOCK

    @pl.when(t == 0)
    def _():
        k_scr[:, 0:BLOCK, :] = jnp.zeros((2 * ATTN_KV_HEADS, BLOCK, LANES), BF16)
        v_scr[:, 0:BLOCK, :] = jnp.zeros((2 * ATTN_KV_HEADS, BLOCK, LANES), BF16)

    x = x_ref[0]
    h = _rms(x, g_ref[...]).astype(BF16)
    qkv = jnp.dot(h, wqkv_ref[...], preferred_element_type=F32) + bqkv_ref[...]
    qk = qkv[:, 0:QK_W]
    ssq = jnp.dot((qk * qk).astype(BF16), seg_ref[...], preferred_element_type=F32)
    r_hi, r_lo = _split2(lax.rsqrt(ssq * (1.0 / ATTN_HEAD_DIM) + EPS))
    r_exp = jnp.dot(jnp.concatenate([r_hi, r_lo], axis=1), exp_ref[...],
                    preferred_element_type=F32)
    qkn = qk * r_exp * gqk_ref[...]
    q_scr[...] = qkn[:, 0:Q_W].astype(BF16)

    lane = lax.broadcasted_iota(jnp.int32, (tm, LANES), 1)
    low_half = lane < HALF
    zeros = jnp.zeros((tm, LANES), F32)
    for src, dst, base in ((qkn, k_scr, Q_W), (qkv, v_scr, QK_W)):
        for pair in range(ATTN_KV_HEADS // 2):
            a2 = src[:, base + pair * LANES:base + (pair + 1) * LANES]
            a2r = pltpu.roll(a2, HALF, axis=1)
            forms = (
                jnp.where(low_half, a2, zeros), jnp.where(low_half, zeros, a2r),
                jnp.where(low_half, a2r, zeros), jnp.where(low_half, zeros, a2),
            )
            for f, val in enumerate(forms):
                dst[4 * pair + f, BLOCK:BLOCK + tm, :] = val.astype(BF16)

    nt_dims = (((1,), (1,)), ((), ()))
    lane_b = lax.broadcasted_iota(jnp.int32, (BLOCK, LANES), 1) < HALF

    def block_body(i, carry):
        r0 = pl.multiple_of(i * BLOCK, BLOCK)
        rows = pl.ds(r0, BLOCK)
        krows = pl.ds(r0, 2 * BLOCK)
        first = jnp.logical_and(t == 0, i == 0)
        sel = jnp.where(first, 1, 0)
        for kvh in range(ATTN_KV_HEADS):
            for pair in range(ATTN_GROUP // 2):
                col0 = kvh * ATTN_GROUP * ATTN_HEAD_DIM + pair * LANES
                q2 = q_scr[rows, col0:col0 + LANES]
                o2 = None
                invs = []
                for parity in range(2):
                    head = kvh * ATTN_GROUP + 2 * pair + parity
                    s = lax.dot_general(q2, k_scr[2 * kvh + parity, krows, :], nt_dims,
                                        preferred_element_type=F32)
                    s = s + bias_ref[sel, head]
                    sink = sink_ref[head]
                    m = jnp.maximum(jnp.max(s, axis=-1, keepdims=True), sink)
                    p = jnp.exp(s - m)
                    denom = jnp.sum(p, axis=-1, keepdims=True) + jnp.exp(sink - m)
                    invs.append(1.0 / denom)
                    part = jnp.dot(p.astype(BF16), v_scr[2 * kvh + parity, krows, :],
                                   preferred_element_type=F32)
                    o2 = part if o2 is None else o2 + part
                att_scr[rows, col0:col0 + LANES] = (
                    o2 * jnp.where(lane_b, invs[0], invs[1])).astype(BF16)
        return carry

    lax.fori_loop(0, n_blocks, block_body, 0)
    k_scr[:, 0:BLOCK, :] = k_scr[:, tm:tm + BLOCK, :]
    v_scr[:, 0:BLOCK, :] = v_scr[:, tm:tm + BLOCK, :]

    o_ref[0] = x + jnp.dot(att_scr[...], wo_ref[...], preferred_element_type=F32) + bo_ref[...]


def _alibi_bias():
    slopes = 2.0 ** (-8.0 * (np.arange(ATTN_HEADS) + 1) / ATTN_HEADS)
    dist = np.arange(BLOCK)[:, None] + BLOCK - np.arange(2 * BLOCK)[None, :]
    in_window = (dist >= 0) & (dist < WINDOW)
    base = -slopes.astype(np.float32)[:, None, None] * dist.astype(np.float32)[None]
    normal = np.where(in_window[None], base, -np.inf)
    has_prev = (np.arange(2 * BLOCK) >= BLOCK)[None, None, :]
    first = np.where(in_window[None] & has_prev, base, -np.inf)
    return jnp.asarray(np.stack([normal, first]).astype(np.float32))


def _segment_matrices():
    seg = np.zeros((QK_W, LANES), np.float32)
    exp = np.zeros((2 * LANES, QK_W), np.float32)
    for hh in range(N_QK_HEADS):
        seg[hh * ATTN_HEAD_DIM:(hh + 1) * ATTN_HEAD_DIM, hh] = 1.0
        exp[hh, hh * ATTN_HEAD_DIM:(hh + 1) * ATTN_HEAD_DIM] = 1.0
        exp[LANES + hh, hh * ATTN_HEAD_DIM:(hh + 1) * ATTN_HEAD_DIM] = 1.0
    return jnp.asarray(seg, BF16), jnp.asarray(exp, BF16)


def _odd_mixer(x3, gain, wqkv, bqkv, qnorm, knorm, sinks, wo, bo):
    b, t, d = x3.shape
    tm = TM_ODD
    assert t % tm == 0 and tm % BLOCK == 0
    qkv_w = wqkv.shape[1]
    gqk = jnp.concatenate([jnp.tile(qnorm, ATTN_HEADS) * (ATTN_HEAD_DIM ** -0.5),
                           jnp.tile(knorm, ATTN_KV_HEADS)]).reshape(1, QK_W)
    seg, exp = _segment_matrices()
    return pl.pallas_call(
        _odd_kernel,
        grid=(b, t // tm),
        in_specs=[
            pl.BlockSpec(memory_space=pltpu.SMEM),
            pl.BlockSpec((1, tm, d), lambda i, j: (i, j, 0)),
            _resident((1, d)),
            _resident((d, qkv_w)),
            _resident((1, qkv_w)),
            _resident((1, QK_W)),
            _resident((QK_W, LANES)),
            _resident((2 * LANES, QK_W)),
            _resident((2, ATTN_HEADS, BLOCK, 2 * BLOCK)),
            _resident((Q_W, d)),
            _resident((1, d)),
        ],
        out_specs=pl.BlockSpec((1, tm, d), lambda i, j: (i, j, 0)),
        out_shape=jax.ShapeDtypeStruct((b, t, d), F32),
        scratch_shapes=[
            pltpu.VMEM((tm, Q_W), BF16),
            pltpu.VMEM((2 * ATTN_KV_HEADS, tm + BLOCK, LANES), BF16),
            pltpu.VMEM((2 * ATTN_KV_HEADS, tm + BLOCK, LANES), BF16),
            pltpu.VMEM((tm, Q_W), BF16),
        ],
        compiler_params=_params(2),
        name="odd_mixer",
    )(sinks, x3, gain.reshape(1, d), wqkv.astype(BF16), bqkv.reshape(1, qkv_w), gqk, seg, exp,
      _alibi_bias(), wo.astype(BF16), bo.reshape(1, d))


def kernel(x, mem, mem_norm, mem_wkv, mem_knorm, ffn1_norm, ffn1_wi, ffn1_wo, mix_norm, ssd_in_proj, ssd_conv_w, ssd_conv_b, ssd_dt_bias, ssd_a_log, ssd_d, ssd_norm, pool_w, pool_scale, even_out_proj, attn_wqkv, attn_bqkv, attn_qnorm, attn_knorm, attn_sinks, attn_wo, attn_bo, xattn_norm, xattn_wq, xattn_qnorm, xattn_wo, ffn2_norm, ffn2_wi, ffn2_wo):
    b, t, d = x.shape
    depth = ffn1_norm.shape[0]
    mem_kt, mem_v = _mem_kv(mem, mem_norm, mem_wkv, mem_knorm)
    for i in range(depth):
        x = _ffn(x.reshape(b * t, d), ffn1_norm[i], ffn1_wi[i], ffn1_wo[i]).reshape(b, t, d)
        if i % 2 == 0:
            e = i // 2
            x = _even_mixer(x, mix_norm[i], ssd_in_proj[e], ssd_conv_w[e], ssd_conv_b[e],
                            ssd_dt_bias[e], ssd_a_log[e], ssd_d[e], ssd_norm[e], pool_w[e],
                            pool_scale[e], even_out_proj[e])
        else:
            o = i // 2
            x = _odd_mixer(x, mix_norm[i], attn_wqkv[o], attn_bqkv[o], attn_qnorm[o], attn_knorm[o],
                           attn_sinks[o], attn_wo[o], attn_bo[o])
        x = _xattn(x, xattn_norm[i], xattn_wq[i], xattn_qnorm[i], mem_kt, mem_v, xattn_wo[i])
        x = _ffn(x.reshape(b * t, d), ffn2_norm[i], ffn2_wi[i], ffn2_wo[i]).reshape(b, t, d)
    return x
```

```python
import numpy as np
import jax
import jax.numpy as jnp
from jax import lax
from jax.experimental import pallas as pl
from jax.experimental.pallas import tpu as pltpu

F32 = jnp.float32
BF16 = jnp.bfloat16

EPS = 1e-6
FFN_RESIDUAL = 0.5
SSD_HEADS = 16
SSD_HEAD_DIM = 64
SSD_GROUPS = 2
D_STATE = 128
CONV_WIDTH = 4
CHUNK = 128
D_SSM = SSD_HEADS * SSD_HEAD_DIM
GROUP_W = D_SSM // SSD_GROUPS
CONV_CH = D_SSM + 2 * SSD_GROUPS * D_STATE
POOL_WINDOWS = (2, 4, 8, 16)
POOL_GROUP = 256
D_POOL = POOL_GROUP * len(POOL_WINDOWS)
POOL_HALO = 16
CONV_HALO = 8
ATTN_HEADS = 16
ATTN_KV_HEADS = 4
ATTN_GROUP = ATTN_HEADS // ATTN_KV_HEADS
ATTN_HEAD_DIM = 64
WINDOW = 128
BLOCK = 128
MEM_HEADS = 4

LANES = 128
SUBLANES = 8
HALF = 64

TM_FFN = 512
TM_XATTN = 512
TM_EVEN = 512
TM_ODD = 512
VMEM_LIMIT_BYTES = 56 * 1024 * 1024
FFN_CHUNKS = (1024, 1024, 768)


def _rms(x, gain):
    ms = jnp.mean(x * x, axis=-1, keepdims=True)
    return x * lax.rsqrt(ms + EPS) * gain


def _silu(x):
    return x * (1.0 / (1.0 + jnp.exp(-x)))


def _split2(x):
    hi = x.astype(BF16)
    lo = (x - hi.astype(F32)).astype(BF16)
    return hi, lo


def _resident(shape):
    nd = len(shape)
    return pl.BlockSpec(shape, lambda *_: (0,) * nd, pipeline_mode=pl.Buffered(1))


def _resident_layer(shape, layer):
    nd = len(shape)
    return pl.BlockSpec((None,) + tuple(shape), lambda *_: (layer,) + (0,) * nd,
                        pipeline_mode=pl.Buffered(1))


def _params(n_axes):
    return pltpu.CompilerParams(
        dimension_semantics=("arbitrary",) * n_axes,
        vmem_limit_bytes=VMEM_LIMIT_BYTES,
    )


def _memkv_kernel(mem_ref, g_ref, wkv_ref, kn_ref, kt_ref, v_ref):
    d = mem_ref.shape[-1]
    hd = d // MEM_HEADS
    h = _rms(mem_ref[0], g_ref[...]).astype(BF16)
    kv = jnp.dot(h, wkv_ref[...], preferred_element_type=F32)
    for i in range(MEM_HEADS):
        kh = _rms(kv[:, i * hd:(i + 1) * hd], kn_ref[...])
        kt_ref[0, i] = kh.T.astype(BF16)
    v_ref[0] = kv[:, d:].astype(BF16)


def _mem_kv(mem, mem_norm, wkv, mem_knorm):
    b, m, d = mem.shape
    hd = d // MEM_HEADS
    return pl.pallas_call(
        _memkv_kernel,
        grid=(b,),
        in_specs=[
            pl.BlockSpec((1, m, d), lambda i: (i, 0, 0)),
            _resident((1, d)),
            _resident((d, 2 * d)),
            _resident((1, hd)),
        ],
        out_specs=[
            pl.BlockSpec((1, MEM_HEADS, hd, m), lambda i: (i, 0, 0, 0)),
            pl.BlockSpec((1, m, d), lambda i: (i, 0, 0)),
        ],
        out_shape=[
            jax.ShapeDtypeStruct((b, MEM_HEADS, hd, m), BF16),
            jax.ShapeDtypeStruct((b, m, d), BF16),
        ],
        compiler_params=_params(1),
        name="mem_kv",
    )(mem, mem_norm.reshape(1, d), wkv.astype(BF16), mem_knorm.reshape(1, hd))


def _ffn_kernel(x_ref, g_ref, wi_ref, wo_ref, o_ref, act_ref):
    d_ff = wo_ref.shape[0]
    x = x_ref[...]
    h = _rms(x, g_ref[...]).astype(BF16)
    lo = 0
    for ck in FFN_CHUNKS:
        gate = jnp.dot(h, wi_ref[:, lo:lo + ck], preferred_element_type=F32)
        up = jnp.dot(h, wi_ref[:, d_ff + lo:d_ff + lo + ck], preferred_element_type=F32)
        act_ref[:, lo:lo + ck] = (_silu(gate) * up).astype(BF16)
        lo += ck
    y = jnp.dot(act_ref[...], wo_ref[...], preferred_element_type=F32)
    o_ref[...] = x + FFN_RESIDUAL * y


def _ffn(x2, gain, wi_all, wo_all, layer):
    n, d = x2.shape
    d_ff = wo_all.shape[1]
    assert sum(FFN_CHUNKS) == d_ff and n % TM_FFN == 0
    return pl.pallas_call(
        _ffn_kernel,
        grid=(n // TM_FFN,),
        in_specs=[
            pl.BlockSpec((TM_FFN, d), lambda i: (i, 0)),
            _resident((1, d)),
            _resident_layer((d, 2 * d_ff), layer),
            _resident_layer((d_ff, d), layer),
        ],
        out_specs=pl.BlockSpec((TM_FFN, d), lambda i: (i, 0)),
        out_shape=jax.ShapeDtypeStruct((n, d), F32),
        scratch_shapes=[pltpu.VMEM((TM_FFN, d_ff), BF16)],
        compiler_params=_params(1),
        name="ffn",
    )(x2, gain.reshape(1, d), wi_all, wo_all)


def _xattn_kernel(x_ref, g_ref, wq_ref, qn_ref, kt_ref, v_ref, wo_ref, o_ref, att_ref):
    d = x_ref.shape[-1]
    hd = d // MEM_HEADS
    x = x_ref[0]
    h = _rms(x, g_ref[...]).astype(BF16)
    q = jnp.dot(h, wq_ref[...], preferred_element_type=F32)
    for i in range(MEM_HEADS):
        cols = slice(i * hd, (i + 1) * hd)
        qh = (_rms(q[:, cols], qn_ref[...]) * (hd ** -0.5)).astype(BF16)
        s = jnp.dot(qh, kt_ref[0, i], preferred_element_type=F32)
        p = jnp.exp(s - jnp.max(s, axis=-1, keepdims=True))
        inv = 1.0 / jnp.sum(p, axis=-1, keepdims=True)
        o = jnp.dot(p.astype(BF16), v_ref[0, :, cols], preferred_element_type=F32)
        att_ref[:, cols] = (o * inv).astype(BF16)
    y = jnp.dot(att_ref[...], wo_ref[...], preferred_element_type=F32)
    o_ref[0] = x + y


def _xattn(x3, gain, wq_all, qnorm, mem_kt, mem_v, wo_all, layer):
    b, t, d = x3.shape
    hd = d // MEM_HEADS
    m = mem_v.shape[1]
    tm = TM_XATTN
    assert t % tm == 0
    return pl.pallas_call(
        _xattn_kernel,
        grid=(b, t // tm),
        in_specs=[
            pl.BlockSpec((1, tm, d), lambda i, j: (i, j, 0)),
            _resident((1, d)),
            _resident_layer((d, d), layer),
            _resident((1, hd)),
            pl.BlockSpec((1, MEM_HEADS, hd, m), lambda i, j: (i, 0, 0, 0)),
            pl.BlockSpec((1, m, d), lambda i, j: (i, 0, 0)),
            _resident_layer((d, d), layer),
        ],
        out_specs=pl.BlockSpec((1, tm, d), lambda i, j: (i, j, 0)),
        out_shape=jax.ShapeDtypeStruct((b, t, d), F32),
        scratch_shapes=[pltpu.VMEM((tm, d), BF16)],
        compiler_params=_params(2),
        name="xattn",
    )(x3, gain.reshape(1, d), wq_all, qnorm.reshape(1, hd), mem_kt, mem_v, wo_all)


CONV_TILES = CONV_CH // LANES
POOL_TILES = D_POOL // LANES
TILES_PER_POOL_GROUP = POOL_GROUP // LANES


def _even_kernel(x_ref, g_ref, win_ref, cw_ref, cb_ref, dtb_ref, alog_ref, dskip_ref, norm_ref,
                 pw_ref, ps_ref, wout_ref, exph_ref, tril_ref, o_ref,
                 conv_scr, act_scr, dt_scr, z_scr, pool_scr, state_scr, conv_halo, pool_halo):
    t = pl.program_id(1)
    tm = x_ref.shape[1]
    n_chunks = tm // CHUNK

    slot = lax.rem(t, 2)

    @pl.when(t == 0)
    def _():
        conv_halo[0] = jnp.zeros((CONV_TILES, CONV_HALO, LANES), F32)
        pool_halo[0] = jnp.zeros((POOL_TILES, POOL_HALO, LANES), F32)
        state_scr[...] = jnp.zeros(state_scr.shape, F32)

    conv_scr[:, 0:CONV_HALO, :] = conv_halo[slot]
    pool_scr[:, 0:POOL_HALO, :] = pool_halo[slot]

    x = x_ref[0]
    h = _rms(x, g_ref[...]).astype(BF16)
    o1 = D_SSM + CONV_CH
    o2 = o1 + D_POOL
    xbc = jnp.dot(h, win_ref[:, D_SSM:o1], preferred_element_type=F32)
    for c in range(CONV_TILES):
        conv_scr[c, CONV_HALO:CONV_HALO + tm, :] = xbc[:, c * LANES:(c + 1) * LANES]
    u_pool = jnp.dot(h, win_ref[:, o1:o2], preferred_element_type=F32)
    for c in range(POOL_TILES):
        pool_scr[c, POOL_HALO:POOL_HALO + tm, :] = u_pool[:, c * LANES:(c + 1) * LANES]
    dt_raw = jnp.dot(h, win_ref[:, o2:o2 + LANES], preferred_element_type=F32) + dtb_ref[...]
    dt_scr[...] = jnp.maximum(dt_raw, 0.0) + jnp.log1p(jnp.exp(-jnp.abs(dt_raw)))
    z_scr[...] = jnp.dot(h, win_ref[:, 0:D_SSM], preferred_element_type=F32)

    for c in range(CONV_TILES):
        cols = slice(c * LANES, (c + 1) * LANES)
        acc = cb_ref[:, cols]
        for k in range(CONV_WIDTH):
            acc = acc + (conv_scr[c, pl.ds(CONV_HALO - CONV_WIDTH + 1 + k, tm), :]
                         * cw_ref[k:k + 1, cols])
        act_scr[:, cols] = _silu(acc)
    conv_halo[1 - slot] = conv_scr[:, tm:tm + CONV_HALO, :]

    pos = t * tm + lax.broadcasted_iota(jnp.int32, (tm, 1), 0)
    y_pool = []
    for k, w in enumerate(POOL_WINDOWS):
        inv_count = 1.0 / jnp.minimum(pos + 1, w).astype(F32)
        pooled = []
        for c in range(k * TILES_PER_POOL_GROUP, (k + 1) * TILES_PER_POOL_GROUP):
            if w > SUBLANES:
                e = pool_scr[c, pl.ds(POOL_HALO - SUBLANES, tm + SUBLANES), :]
                for i in range(1, SUBLANES):
                    e = e + pool_scr[c, pl.ds(POOL_HALO - SUBLANES - i, tm + SUBLANES), :]
                s = e[SUBLANES:, :] + e[:tm, :]
            else:
                s = pool_scr[c, pl.ds(POOL_HALO, tm), :]
                for i in range(1, w):
                    s = s + pool_scr[c, pl.ds(POOL_HALO - i, tm), :]
            pooled.append(s * inv_count - pool_scr[c, pl.ds(POOL_HALO, tm), :])
        pooled = jnp.concatenate(pooled, axis=1).astype(BF16)
        kcols = slice(k * POOL_GROUP, (k + 1) * POOL_GROUP)
        yk = jnp.dot(pooled, pw_ref[k], preferred_element_type=F32)
        y_pool.append((yk * ps_ref[:, kcols]).astype(BF16))
    pool_halo[1 - slot] = pool_scr[:, tm:tm + POOL_HALO, :]
    o_ref[0] = x + jnp.dot(jnp.concatenate(y_pool, axis=1), wout_ref[D_SSM:, :],
                           preferred_element_type=F32)

    a_row = -jnp.exp(alog_ref[...])
    low_half = (lax.broadcasted_iota(jnp.int32, (CHUNK, D_SSM), 1) & HALF) == 0
    causal = (lax.broadcasted_iota(jnp.int32, (CHUNK, CHUNK), 0)
              >= lax.broadcasted_iota(jnp.int32, (CHUNK, CHUNK), 1))
    heads_per_group = SSD_HEADS // SSD_GROUPS

    def chunk_body(c, carry):
        r0 = pl.multiple_of(c * CHUNK, CHUNK)
        rows = pl.ds(r0, CHUNK)
        xs = act_scr[rows, 0:D_SSM]
        dt = dt_scr[rows, :]
        adt = dt * a_row
        hi = adt.astype(BF16)
        r1 = adt - hi.astype(F32)
        mid = r1.astype(BF16)
        lo = (r1 - mid.astype(F32)).astype(BF16)
        acs = jnp.dot(tril_ref[...], jnp.concatenate([hi, mid, lo], axis=0),
                      preferred_element_type=F32)
        acs_last = acs[CHUNK - 1:CHUNK, :]
        stacked = jnp.concatenate([dt, jnp.exp(acs_last - acs), jnp.exp(acs)], axis=0)
        s_hi, s_lo = _split2(stacked)
        expanded = jnp.dot(jnp.concatenate([s_hi, s_lo], axis=1), exph_ref[...],
                           preferred_element_type=F32)
        dt_exp = expanded[0:CHUNK]
        dte_exp = expanded[CHUNK:2 * CHUNK]
        dfs_exp = expanded[2 * CHUNK:3 * CHUNK]
        xdt = xs * dt_exp
        xw_b = (xdt * dte_exp).astype(BF16)
        x_even = jnp.where(low_half, xdt, 0.0).astype(BF16)
        x_odd = jnp.where(low_half, 0.0, xdt).astype(BF16)
        acs_row = acs.T

        y_tiles = []
        for g in range(SSD_GROUPS):
            gcols = slice(g * GROUP_W, (g + 1) * GROUP_W)
            b_g = act_scr[rows, D_SSM + g * D_STATE:D_SSM + (g + 1) * D_STATE]
            c_g = act_scr[rows, D_SSM + SSD_GROUPS * D_STATE + g * D_STATE:
                          D_SSM + SSD_GROUPS * D_STATE + (g + 1) * D_STATE].astype(BF16)
            bt_g = b_g.T.astype(BF16)
            cb = jnp.dot(c_g, bt_g, preferred_element_type=F32)
            s_new = jnp.dot(bt_g, xw_b[:, gcols], preferred_element_type=F32)
            s_in = state_scr[g]
            y_off = jnp.dot(c_g, s_in.astype(BF16), preferred_element_type=F32) * dfs_exp[:, gcols]
            state_scr[g] = s_in * dfs_exp[CHUNK - 1:CHUNK, gcols] + s_new
            for pair in range(heads_per_group // 2):
                pcols = slice(g * GROUP_W + pair * LANES, g * GROUP_W + (pair + 1) * LANES)
                y2 = y_off[:, pair * LANES:(pair + 1) * LANES]
                for parity, x_par in enumerate((x_even, x_odd)):
                    hh = g * heads_per_group + 2 * pair + parity
                    seg = jnp.where(causal, jnp.exp(acs[:, hh:hh + 1] - acs_row[hh:hh + 1, :]), 0.0)
                    w = (cb * seg).astype(BF16)
                    y2 = y2 + jnp.dot(w, x_par[:, pcols], preferred_element_type=F32)
                y_tiles.append(y2)
        y = (jnp.concatenate(y_tiles, axis=1) + dskip_ref[...] * xs) * _silu(z_scr[rows, :])
        y_n = jnp.concatenate(
            [_rms(y[:, g * GROUP_W:(g + 1) * GROUP_W], norm_ref[:, g * GROUP_W:(g + 1) * GROUP_W])
             for g in range(SSD_GROUPS)], axis=1).astype(BF16)
        o_ref[0, rows, :] = o_ref[0, rows, :] + jnp.dot(y_n, wout_ref[0:D_SSM, :],
                                                        preferred_element_type=F32)
        return carry

    lax.fori_loop(0, n_chunks, chunk_body, 0, unroll=True)


def _head_expand_matrix():
    m = np.zeros((2 * LANES, D_SSM), np.float32)
    for hh in range(SSD_HEADS):
        m[hh, hh * SSD_HEAD_DIM:(hh + 1) * SSD_HEAD_DIM] = 1.0
        m[LANES + hh, hh * SSD_HEAD_DIM:(hh + 1) * SSD_HEAD_DIM] = 1.0
    return jnp.asarray(m, BF16)


def _even_mixer(x3, gain, in_proj, conv_w, conv_b, dt_bias, a_log, d_skip, ssd_norm,
                pool_w, pool_scale, out_proj):
    b, t, d = x3.shape
    tm = TM_EVEN
    assert t % tm == 0 and tm % CHUNK == 0
    o_dt = D_SSM + CONV_CH
    o_pool = o_dt + SSD_HEADS
    win = jnp.concatenate(
        [in_proj[:, :o_dt], in_proj[:, o_pool:], in_proj[:, o_dt:o_pool],
         jnp.zeros((d, LANES - SSD_HEADS), in_proj.dtype)], axis=1).astype(BF16)
    n_in = win.shape[1]
    pad = LANES - SSD_HEADS
    dtb = jnp.pad(dt_bias, (0, pad)).reshape(1, LANES)
    alog = jnp.pad(a_log, (0, pad)).reshape(1, LANES)
    dskip = jnp.repeat(d_skip, SSD_HEAD_DIM).reshape(1, D_SSM)
    tril3 = jnp.asarray(np.tile(np.tril(np.ones((CHUNK, CHUNK), np.float32)), (1, 3)), BF16)
    n_g = len(POOL_WINDOWS)
    return pl.pallas_call(
        _even_kernel,
        grid=(b, t // tm),
        in_specs=[
            pl.BlockSpec((1, tm, d), lambda i, j: (i, j, 0)),
            _resident((1, d)),
            _resident((d, n_in)),
            _resident((CONV_WIDTH, CONV_CH)),
            _resident((1, CONV_CH)),
            _resident((1, LANES)),
            _resident((1, LANES)),
            _resident((1, D_SSM)),
            _resident((1, D_SSM)),
            _resident((n_g, POOL_GROUP, POOL_GROUP)),
            _resident((1, D_POOL)),
            _resident((D_SSM + D_POOL, d)),
            _resident((2 * LANES, D_SSM)),
            _resident((CHUNK, 3 * CHUNK)),
        ],
        out_specs=pl.BlockSpec((1, tm, d), lambda i, j: (i, j, 0)),
        out_shape=jax.ShapeDtypeStruct((b, t, d), F32),
        scratch_shapes=[
            pltpu.VMEM((CONV_TILES, tm + CONV_HALO, LANES), F32),
            pltpu.VMEM((tm, CONV_CH), F32),
            pltpu.VMEM((tm, LANES), F32),
            pltpu.VMEM((tm, D_SSM), F32),
            pltpu.VMEM((POOL_TILES, tm + POOL_HALO, LANES), F32),
            pltpu.VMEM((SSD_GROUPS, D_STATE, GROUP_W), F32),
            pltpu.VMEM((2, CONV_TILES, CONV_HALO, LANES), F32),
            pltpu.VMEM((2, POOL_TILES, POOL_HALO, LANES), F32),
        ],
        compiler_params=_params(2),
        name="even_mixer",
    )(x3, gain.reshape(1, d), win, conv_w, conv_b.reshape(1, CONV_CH), dtb, alog, dskip,
      ssd_norm.reshape(1, D_SSM), pool_w.astype(BF16), pool_scale.reshape(1, D_POOL),
      out_proj.astype(BF16), _head_expand_matrix(), tril3)


N_QK_HEADS = ATTN_HEADS + ATTN_KV_HEADS
QK_W = N_QK_HEADS * ATTN_HEAD_DIM
Q_W = ATTN_HEADS * ATTN_HEAD_DIM
KV_W = ATTN_KV_HEADS * ATTN_HEAD_DIM
PAIRS = ATTN_GROUP // 2
STACK = PAIRS * BLOCK


def _odd_kernel(sink_ref, x_ref, g_ref, wqkv_ref, bqkv_ref, gqk_ref, seg_ref, exp_ref, bias_ref,
                wo_ref, bo_ref, o_ref, q_scr, k_scr, v_scr, k_prev, v_prev):
    t = pl.program_id(1)
    tm = x_ref.shape[1]
    n_blocks = tm // BLOCK
    n_forms = 2 * ATTN_KV_HEADS

    slot = lax.rem(t, 2)

    @pl.when(t == 0)
    def _():
        k_prev[0] = jnp.zeros((n_forms, BLOCK, LANES), BF16)
        v_prev[0] = jnp.zeros((n_forms, BLOCK, 2 * LANES), BF16)

    k_scr[:, 0:BLOCK, :] = k_prev[slot]
    v_scr[:, 0:BLOCK, :] = v_prev[slot]

    x = x_ref[0]
    h = _rms(x, g_ref[...]).astype(BF16)
    qkv = jnp.dot(h, wqkv_ref[...], preferred_element_type=F32) + bqkv_ref[...]
    qk = qkv[:, 0:QK_W]
    ssq = jnp.dot((qk * qk).astype(BF16), seg_ref[...], preferred_element_type=F32)
    r_hi, r_lo = _split2(lax.rsqrt(ssq * (1.0 / ATTN_HEAD_DIM) + EPS))
    r_exp = jnp.dot(jnp.concatenate([r_hi, r_lo], axis=1), exp_ref[...],
                    preferred_element_type=F32)
    qkn = qk * r_exp * gqk_ref[...]
    q_scr[...] = qkn[:, 0:Q_W].astype(BF16)

    low_half = lax.broadcasted_iota(jnp.int32, (tm, LANES), 1) < HALF
    zeros = jnp.zeros((tm, LANES), F32)
    ones_lo = jnp.where(low_half, 1.0, 0.0).astype(BF16)
    ones_hi = jnp.where(low_half, 0.0, 1.0).astype(BF16)
    for is_v, src, dst, base in ((False, qkn, k_scr, Q_W), (True, qkv, v_scr, QK_W)):
        for pair in range(ATTN_KV_HEADS // 2):
            a2 = src[:, base + pair * LANES:base + (pair + 1) * LANES]
            a2r = pltpu.roll(a2, HALF, axis=1)
            forms = (
                jnp.where(low_half, a2, zeros), jnp.where(low_half, zeros, a2r),
                jnp.where(low_half, a2r, zeros), jnp.where(low_half, zeros, a2),
            )
            for f, val in enumerate(forms):
                dst[4 * pair + f, BLOCK:BLOCK + tm, 0:LANES] = val.astype(BF16)
                if is_v:
                    dst[4 * pair + f, BLOCK:BLOCK + tm, LANES:2 * LANES] = (
                        ones_lo if f % 2 == 0 else ones_hi)

    nt_dims = (((1,), (1,)), ((), ()))
    row = lax.broadcasted_iota(jnp.int32, (STACK, LANES), 0)
    col = lax.broadcasted_iota(jnp.int32, (STACK, LANES), 1)
    tri = col <= (row & (BLOCK - 1))
    lane_low = col < HALF
    row_pair0 = lax.broadcasted_iota(jnp.int32, (STACK, 1), 0) < BLOCK

    def block_body(i, carry):
        r0 = pl.multiple_of(i * BLOCK, BLOCK)
        rows = pl.ds(r0, BLOCK)
        krows = pl.ds(r0, 2 * BLOCK)
        sel = jnp.where(jnp.logical_and(t == 0, i == 0), 1, 0)
        att_tiles = []
        for kvh in range(ATTN_KV_HEADS):
            col0 = kvh * ATTN_GROUP * ATTN_HEAD_DIM
            q2 = jnp.concatenate(
                [q_scr[rows, col0 + p * LANES:col0 + (p + 1) * LANES] for p in range(PAIRS)],
                axis=0)
            acc = None
            shifts = []
            for parity in range(2):
                s2 = lax.dot_general(q2, k_scr[2 * kvh + parity, krows, :], nt_dims,
                                     preferred_element_type=F32)
                sc = jnp.where(tri, s2[:, BLOCK:], s2[:, :BLOCK]) + bias_ref[sel, kvh, parity]
                sink = jnp.where(row_pair0, sink_ref[kvh * ATTN_GROUP + parity],
                                 sink_ref[kvh * ATTN_GROUP + 2 + parity])
                m = jnp.maximum(jnp.max(sc, axis=-1, keepdims=True), sink)
                p = jnp.exp(sc - m)
                pcat = jnp.concatenate([jnp.where(tri, 0.0, p), jnp.where(tri, p, 0.0)],
                                       axis=1).astype(BF16)
                part = jnp.dot(pcat, v_scr[2 * kvh + parity, krows, :],
                               preferred_element_type=F32)
                acc = part if acc is None else acc + part
                shifts.append(sink - m)
            denom = acc[:, LANES:] + jnp.exp(jnp.where(lane_low, shifts[0], shifts[1]))
            att = (acc[:, :LANES] * (1.0 / denom)).astype(BF16)
            att_tiles.extend(att[p * BLOCK:(p + 1) * BLOCK] for p in range(PAIRS))
        o_ref[0, rows, :] = (x_ref[0, rows, :] + bo_ref[...]
                             + jnp.dot(jnp.concatenate(att_tiles, axis=1), wo_ref[...],
                                       preferred_element_type=F32))
        return carry

    lax.fori_loop(0, n_blocks, block_body, 0, unroll=True)
    k_prev[1 - slot] = k_scr[:, tm:tm + BLOCK, :]
    v_prev[1 - slot] = v_scr[:, tm:tm + BLOCK, :]


def _alibi_bias():
    slopes = 2.0 ** (-8.0 * (np.arange(ATTN_HEADS) + 1) / ATTN_HEADS)
    q = np.arange(BLOCK)[:, None]
    j = np.arange(BLOCK)[None, :]
    cur = j <= q
    dist = np.where(cur, q - j, q - j + BLOCK).astype(np.float32)
    assert dist.min() >= 0 and dist.max() < WINDOW
    out = np.zeros((2, ATTN_KV_HEADS, 2, STACK, BLOCK), np.float32)
    for kvh in range(ATTN_KV_HEADS):
        for parity in range(2):
            for p in range(PAIRS):
                head = kvh * ATTN_GROUP + 2 * p + parity
                base = -np.float32(slopes[head]) * dist
                out[0, kvh, parity, p * BLOCK:(p + 1) * BLOCK] = base
                out[1, kvh, parity, p * BLOCK:(p + 1) * BLOCK] = np.where(cur, base, -np.inf)
    return jnp.asarray(out)


def _segment_matrices():
    seg = np.zeros((QK_W, LANES), np.float32)
    exp = np.zeros((2 * LANES, QK_W), np.float32)
    for hh in range(N_QK_HEADS):
        seg[hh * ATTN_HEAD_DIM:(hh + 1) * ATTN_HEAD_DIM, hh] = 1.0
        exp[hh, hh * ATTN_HEAD_DIM:(hh + 1) * ATTN_HEAD_DIM] = 1.0
        exp[LANES + hh, hh * ATTN_HEAD_DIM:(hh + 1) * ATTN_HEAD_DIM] = 1.0
    return jnp.asarray(seg, BF16), jnp.asarray(exp, BF16)


def _odd_mixer(x3, gain, wqkv, bqkv, qnorm, knorm, sinks, wo, bo):
    b, t, d = x3.shape
    tm = TM_ODD
    assert t % tm == 0 and tm % BLOCK == 0
    qkv_w = wqkv.shape[1]
    gqk = jnp.concatenate([jnp.tile(qnorm, ATTN_HEADS) * (ATTN_HEAD_DIM ** -0.5),
                           jnp.tile(knorm, ATTN_KV_HEADS)]).reshape(1, QK_W)
    seg, exp = _segment_matrices()
    n_forms = 2 * ATTN_KV_HEADS
    return pl.pallas_call(
        _odd_kernel,
        grid=(b, t // tm),
        in_specs=[
            pl.BlockSpec(memory_space=pltpu.SMEM),
            pl.BlockSpec((1, tm, d), lambda i, j: (i, j, 0)),
            _resident((1, d)),
            _resident((d, qkv_w)),
            _resident((1, qkv_w)),
            _resident((1, QK_W)),
            _resident((QK_W, LANES)),
            _resident((2 * LANES, QK_W)),
            _resident((2, ATTN_KV_HEADS, 2, STACK, BLOCK)),
            _resident((Q_W, d)),
            _resident((1, d)),
        ],
        out_specs=pl.BlockSpec((1, tm, d), lambda i, j: (i, j, 0)),
        out_shape=jax.ShapeDtypeStruct((b, t, d), F32),
        scratch_shapes=[
            pltpu.VMEM((tm, Q_W), BF16),
            pltpu.VMEM((n_forms, tm + BLOCK, LANES), BF16),
            pltpu.VMEM((n_forms, tm + BLOCK, 2 * LANES), BF16),
            pltpu.VMEM((2, n_forms, BLOCK, LANES), BF16),
            pltpu.VMEM((2, n_forms, BLOCK, 2 * LANES), BF16),
        ],
        compiler_params=_params(2),
        name="odd_mixer",
    )(sinks, x3, gain.reshape(1, d), wqkv.astype(BF16), bqkv.reshape(1, qkv_w), gqk, seg, exp,
      _alibi_bias(), wo.astype(BF16), bo.reshape(1, d))


def kernel(x, mem, mem_norm, mem_wkv, mem_knorm, ffn1_norm, ffn1_wi, ffn1_wo, mix_norm, ssd_in_proj, ssd_conv_w, ssd_conv_b, ssd_dt_bias, ssd_a_log, ssd_d, ssd_norm, pool_w, pool_scale, even_out_proj, attn_wqkv, attn_bqkv, attn_qnorm, attn_knorm, attn_sinks, attn_wo, attn_bo, xattn_norm, xattn_wq, xattn_qnorm, xattn_wo, ffn2_norm, ffn2_wi, ffn2_wo):
    b, t, d = x.shape
    depth = ffn1_norm.shape[0]
    mem_kt, mem_v = _mem_kv(mem, mem_norm, mem_wkv, mem_knorm)
    wi1, wo1 = ffn1_wi.astype(BF16), ffn1_wo.astype(BF16)
    wi2, wo2 = ffn2_wi.astype(BF16), ffn2_wo.astype(BF16)
    xq, xo = xattn_wq.astype(BF16), xattn_wo.astype(BF16)
    for i in range(depth):
        x = _ffn(x.reshape(b * t, d), ffn1_norm[i], wi1, wo1, i).reshape(b, t, d)
        if i % 2 == 0:
            e = i // 2
            x = _even_mixer(x, mix_norm[i], ssd_in_proj[e], ssd_conv_w[e], ssd_conv_b[e],
                            ssd_dt_bias[e], ssd_a_log[e], ssd_d[e], ssd_norm[e], pool_w[e],
                            pool_scale[e], even_out_proj[e])
        else:
            o = i // 2
            x = _odd_mixer(x, mix_norm[i], attn_wqkv[o], attn_bqkv[o], attn_qnorm[o], attn_knorm[o],
                           attn_sinks[o], attn_wo[o], attn_bo[o])
        x = _xattn(x, xattn_norm[i], xq, xattn_qnorm[i], mem_kt, mem_v, xo, i)
        x = _ffn(x.reshape(b * t, d), ffn2_norm[i], wi2, wo2, i).reshape(b, t, d)
    return x
```

```python
import numpy as np
import jax
import jax.numpy as jnp
from jax import lax
from jax.experimental import pallas as pl
from jax.experimental.pallas import tpu as pltpu

F32 = jnp.float32
BF16 = jnp.bfloat16

EPS = 1e-6
FFN_RESIDUAL = 0.5
SSD_HEADS = 16
SSD_HEAD_DIM = 64
SSD_GROUPS = 2
D_STATE = 128
CONV_WIDTH = 4
CHUNK = 128
D_SSM = SSD_HEADS * SSD_HEAD_DIM
GROUP_W = D_SSM // SSD_GROUPS
CONV_CH = D_SSM + 2 * SSD_GROUPS * D_STATE
POOL_WINDOWS = (2, 4, 8, 16)
POOL_GROUP = 256
D_POOL = POOL_GROUP * len(POOL_WINDOWS)
POOL_HALO = 16
CONV_HALO = 8
ATTN_HEADS = 16
ATTN_KV_HEADS = 4
ATTN_GROUP = ATTN_HEADS // ATTN_KV_HEADS
ATTN_HEAD_DIM = 64
WINDOW = 128
BLOCK = 128
MEM_HEADS = 4

LANES = 128
SUBLANES = 8
HALF = 64

TM_FFN = 512
TM_XATTN = 512
TM_EVEN = 512
TM_ODD = 512
VMEM_LIMIT_BYTES = 56 * 1024 * 1024
FFN_CHUNKS = (1024, 1024, 768)


def _rms(x, gain):
    ms = jnp.mean(x * x, axis=-1, keepdims=True)
    return x * lax.rsqrt(ms + EPS) * gain


def _silu(x):
    return x * (1.0 / (1.0 + jnp.exp(-x)))


def _split2(x):
    hi = x.astype(BF16)
    lo = (x - hi.astype(F32)).astype(BF16)
    return hi, lo


def _resident(shape):
    nd = len(shape)
    return pl.BlockSpec(shape, lambda *_: (0,) * nd, pipeline_mode=pl.Buffered(1))


def _resident_layer(shape, layer):
    nd = len(shape)
    return pl.BlockSpec((None,) + tuple(shape), lambda *_: (layer,) + (0,) * nd,
                        pipeline_mode=pl.Buffered(1))


def _params(n_axes):
    return pltpu.CompilerParams(
        dimension_semantics=("arbitrary",) * n_axes,
        vmem_limit_bytes=VMEM_LIMIT_BYTES,
    )


def _memkv_kernel(mem_ref, g_ref, wkv_ref, kn_ref, kt_ref, v_ref):
    d = mem_ref.shape[-1]
    hd = d // MEM_HEADS
    h = _rms(mem_ref[0], g_ref[...]).astype(BF16)
    kv = jnp.dot(h, wkv_ref[...], preferred_element_type=F32)
    for i in range(MEM_HEADS):
        kh = _rms(kv[:, i * hd:(i + 1) * hd], kn_ref[...])
        kt_ref[0, i] = kh.T.astype(BF16)
    v_ref[0] = kv[:, d:].astype(BF16)


def _mem_kv(mem, mem_norm, wkv, mem_knorm):
    b, m, d = mem.shape
    hd = d // MEM_HEADS
    return pl.pallas_call(
        _memkv_kernel,
        grid=(b,),
        in_specs=[
            pl.BlockSpec((1, m, d), lambda i: (i, 0, 0)),
            _resident((1, d)),
            _resident((d, 2 * d)),
            _resident((1, hd)),
        ],
        out_specs=[
            pl.BlockSpec((1, MEM_HEADS, hd, m), lambda i: (i, 0, 0, 0)),
            pl.BlockSpec((1, m, d), lambda i: (i, 0, 0)),
        ],
        out_shape=[
            jax.ShapeDtypeStruct((b, MEM_HEADS, hd, m), BF16),
            jax.ShapeDtypeStruct((b, m, d), BF16),
        ],
        compiler_params=_params(1),
        name="mem_kv",
    )(mem, mem_norm.reshape(1, d), wkv.astype(BF16), mem_knorm.reshape(1, hd))


def _ffn_kernel(x_ref, g_ref, wi_ref, wo_ref, o_ref, act_ref):
    d_ff = wo_ref.shape[0]
    h = _rms(x_ref[...], g_ref[...]).astype(BF16)
    lo = 0
    for ck in FFN_CHUNKS:
        gate = jnp.dot(h, wi_ref[:, lo:lo + ck], preferred_element_type=F32)
        up = jnp.dot(h, wi_ref[:, d_ff + lo:d_ff + lo + ck], preferred_element_type=F32)
        act_ref[:, lo:lo + ck] = (_silu(gate) * up).astype(BF16)
        lo += ck
    y = jnp.dot(act_ref[...], wo_ref[...], preferred_element_type=F32)
    o_ref[...] = x_ref[...] + FFN_RESIDUAL * y


def _ffn(x2, gain, wi_all, wo_all, layer):
    n, d = x2.shape
    d_ff = wo_all.shape[1]
    assert sum(FFN_CHUNKS) == d_ff and n % TM_FFN == 0
    return pl.pallas_call(
        _ffn_kernel,
        grid=(n // TM_FFN,),
        in_specs=[
            pl.BlockSpec((TM_FFN, d), lambda i: (i, 0)),
            _resident((1, d)),
            _resident_layer((d, 2 * d_ff), layer),
            _resident_layer((d_ff, d), layer),
        ],
        out_specs=pl.BlockSpec((TM_FFN, d), lambda i: (i, 0)),
        out_shape=jax.ShapeDtypeStruct((n, d), F32),
        scratch_shapes=[pltpu.VMEM((TM_FFN, d_ff), BF16)],
        compiler_params=_params(1),
        name="ffn",
    )(x2, gain.reshape(1, d), wi_all, wo_all)


def _xattn_kernel(x_ref, g_ref, wq_ref, qn_ref, kt_ref, v_ref, wo_ref, o_ref, att_ref):
    d = x_ref.shape[-1]
    hd = d // MEM_HEADS
    h = _rms(x_ref[0], g_ref[...]).astype(BF16)
    q = jnp.dot(h, wq_ref[...], preferred_element_type=F32)
    for i in range(MEM_HEADS):
        cols = slice(i * hd, (i + 1) * hd)
        qh = (_rms(q[:, cols], qn_ref[...]) * (hd ** -0.5)).astype(BF16)
        s = jnp.dot(qh, kt_ref[0, i], preferred_element_type=F32)
        p = jnp.exp(s - jnp.max(s, axis=-1, keepdims=True))
        inv = 1.0 / jnp.sum(p, axis=-1, keepdims=True)
        o = jnp.dot(p.astype(BF16), v_ref[0, :, cols], preferred_element_type=F32)
        att_ref[:, cols] = (o * inv).astype(BF16)
    y = jnp.dot(att_ref[...], wo_ref[...], preferred_element_type=F32)
    o_ref[0] = x_ref[0] + y


def _xattn(x3, gain, wq_all, qnorm, mem_kt, mem_v, wo_all, layer):
    b, t, d = x3.shape
    hd = d // MEM_HEADS
    m = mem_v.shape[1]
    tm = TM_XATTN
    assert t % tm == 0
    return pl.pallas_call(
        _xattn_kernel,
        grid=(b, t // tm),
        in_specs=[
            pl.BlockSpec((1, tm, d), lambda i, j: (i, j, 0)),
            _resident((1, d)),
            _resident_layer((d, d), layer),
            _resident((1, hd)),
            pl.BlockSpec((1, MEM_HEADS, hd, m), lambda i, j: (i, 0, 0, 0)),
            pl.BlockSpec((1, m, d), lambda i, j: (i, 0, 0)),
            _resident_layer((d, d), layer),
        ],
        out_specs=pl.BlockSpec((1, tm, d), lambda i, j: (i, j, 0)),
        out_shape=jax.ShapeDtypeStruct((b, t, d), F32),
        scratch_shapes=[pltpu.VMEM((tm, d), BF16)],
        compiler_params=_params(2),
        name="xattn",
    )(x3, gain.reshape(1, d), wq_all, qnorm.reshape(1, hd), mem_kt, mem_v, wo_all)


CONV_TILES = CONV_CH // LANES
POOL_TILES = D_POOL // LANES
TILES_PER_POOL_GROUP = POOL_GROUP // LANES


def _even_kernel(x_ref, g_ref, win_ref, cw_ref, cb_ref, dtb_ref, alog_ref, dskip_ref, norm_ref,
                 pw_ref, ps_ref, wout_ref, exph_ref, tril_ref, o_ref,
                 conv_scr, act_scr, dt_scr, z_scr, pool_scr, state_scr, conv_halo, pool_halo):
    t = pl.program_id(1)
    tm = x_ref.shape[1]
    n_chunks = tm // CHUNK

    slot = lax.rem(t, 2)

    @pl.when(t == 0)
    def _():
        conv_halo[0] = jnp.zeros((CONV_TILES, CONV_HALO, LANES), F32)
        pool_halo[0] = jnp.zeros((POOL_TILES, POOL_HALO, LANES), F32)
        state_scr[...] = jnp.zeros(state_scr.shape, F32)

    conv_scr[:, 0:CONV_HALO, :] = conv_halo[slot]
    pool_scr[:, 0:POOL_HALO, :] = pool_halo[slot]

    x = x_ref[0]
    h = _rms(x, g_ref[...]).astype(BF16)
    o1 = D_SSM + CONV_CH
    o2 = o1 + D_POOL
    xbc = jnp.dot(h, win_ref[:, D_SSM:o1], preferred_element_type=F32)
    for c in range(CONV_TILES):
        conv_scr[c, CONV_HALO:CONV_HALO + tm, :] = xbc[:, c * LANES:(c + 1) * LANES]
    u_pool = jnp.dot(h, win_ref[:, o1:o2], preferred_element_type=F32)
    for c in range(POOL_TILES):
        pool_scr[c, POOL_HALO:POOL_HALO + tm, :] = u_pool[:, c * LANES:(c + 1) * LANES]
    dt_raw = jnp.dot(h, win_ref[:, o2:o2 + LANES], preferred_element_type=F32) + dtb_ref[...]
    dt_scr[...] = jnp.maximum(dt_raw, 0.0) + jnp.log1p(jnp.exp(-jnp.abs(dt_raw)))
    z_scr[...] = jnp.dot(h, win_ref[:, 0:D_SSM], preferred_element_type=F32)

    for c in range(CONV_TILES):
        cols = slice(c * LANES, (c + 1) * LANES)
        acc = cb_ref[:, cols]
        for k in range(CONV_WIDTH):
            acc = acc + (conv_scr[c, pl.ds(CONV_HALO - CONV_WIDTH + 1 + k, tm), :]
                         * cw_ref[k:k + 1, cols])
        act_scr[:, cols] = _silu(acc)
    conv_halo[1 - slot] = conv_scr[:, tm:tm + CONV_HALO, :]

    pos = t * tm + lax.broadcasted_iota(jnp.int32, (tm, 1), 0)
    y_pool = []
    for k, w in enumerate(POOL_WINDOWS):
        inv_count = 1.0 / jnp.minimum(pos + 1, w).astype(F32)
        pooled = []
        for c in range(k * TILES_PER_POOL_GROUP, (k + 1) * TILES_PER_POOL_GROUP):
            if w > SUBLANES:
                e = pool_scr[c, pl.ds(POOL_HALO - SUBLANES, tm + SUBLANES), :]
                for i in range(1, SUBLANES):
                    e = e + pool_scr[c, pl.ds(POOL_HALO - SUBLANES - i, tm + SUBLANES), :]
                s = e[SUBLANES:, :] + e[:tm, :]
            else:
                s = pool_scr[c, pl.ds(POOL_HALO, tm), :]
                for i in range(1, w):
                    s = s + pool_scr[c, pl.ds(POOL_HALO - i, tm), :]
            pooled.append(s * inv_count - pool_scr[c, pl.ds(POOL_HALO, tm), :])
        pooled = jnp.concatenate(pooled, axis=1).astype(BF16)
        kcols = slice(k * POOL_GROUP, (k + 1) * POOL_GROUP)
        yk = jnp.dot(pooled, pw_ref[k], preferred_element_type=F32)
        y_pool.append((yk * ps_ref[:, kcols]).astype(BF16))
    pool_halo[1 - slot] = pool_scr[:, tm:tm + POOL_HALO, :]

    a_row = -jnp.exp(alog_ref[...])
    low_half = (lax.broadcasted_iota(jnp.int32, (CHUNK, D_SSM), 1) & HALF) == 0
    causal = (lax.broadcasted_iota(jnp.int32, (CHUNK, CHUNK), 0)
              >= lax.broadcasted_iota(jnp.int32, (CHUNK, CHUNK), 1))
    heads_per_group = SSD_HEADS // SSD_GROUPS

    def chunk_body(c):
        rows = pl.ds(c * CHUNK, CHUNK)
        xs = act_scr[rows, 0:D_SSM]
        dt = dt_scr[rows, :]
        adt = dt * a_row
        hi = adt.astype(BF16)
        r1 = adt - hi.astype(F32)
        mid = r1.astype(BF16)
        lo = (r1 - mid.astype(F32)).astype(BF16)
        acs = jnp.dot(tril_ref[...], jnp.concatenate([hi, mid, lo], axis=0),
                      preferred_element_type=F32)
        acs_last = acs[CHUNK - 1:CHUNK, :]
        stacked = jnp.concatenate([dt, jnp.exp(acs_last - acs), jnp.exp(acs)], axis=0)
        s_hi, s_lo = _split2(stacked)
        expanded = jnp.dot(jnp.concatenate([s_hi, s_lo], axis=1), exph_ref[...],
                           preferred_element_type=F32)
        dt_exp = expanded[0:CHUNK]
        dte_exp = expanded[CHUNK:2 * CHUNK]
        dfs_exp = expanded[2 * CHUNK:3 * CHUNK]
        xdt = xs * dt_exp
        xw_b = (xdt * dte_exp).astype(BF16)
        x_even = jnp.where(low_half, xdt, 0.0).astype(BF16)
        x_odd = jnp.where(low_half, 0.0, xdt).astype(BF16)
        acs_row = acs.T

        y_tiles = []
        for g in range(SSD_GROUPS):
            gcols = slice(g * GROUP_W, (g + 1) * GROUP_W)
            b_g = act_scr[rows, D_SSM + g * D_STATE:D_SSM + (g + 1) * D_STATE]
            c_g = act_scr[rows, D_SSM + SSD_GROUPS * D_STATE + g * D_STATE:
                          D_SSM + SSD_GROUPS * D_STATE + (g + 1) * D_STATE].astype(BF16)
            bt_g = b_g.T.astype(BF16)
            cb = jnp.dot(c_g, bt_g, preferred_element_type=F32)
            s_new = jnp.dot(bt_g, xw_b[:, gcols], preferred_element_type=F32)
            s_in = state_scr[g]
            y_off = jnp.dot(c_g, s_in.astype(BF16), preferred_element_type=F32) * dfs_exp[:, gcols]
            state_scr[g] = s_in * dfs_exp[CHUNK - 1:CHUNK, gcols] + s_new
            for pair in range(heads_per_group // 2):
                pcols = slice(g * GROUP_W + pair * LANES, g * GROUP_W + (pair + 1) * LANES)
                w2 = []
                for parity in range(2):
                    hh = g * heads_per_group + 2 * pair + parity
                    seg = jnp.where(causal, jnp.exp(acs[:, hh:hh + 1] - acs_row[hh:hh + 1, :]), 0.0)
                    w2.append((cb * seg).astype(BF16))
                y_diag = jnp.dot(jnp.concatenate(w2, axis=1),
                                 jnp.concatenate([x_even[:, pcols], x_odd[:, pcols]], axis=0),
                                 preferred_element_type=F32)
                y_tiles.append(y_diag + y_off[:, pair * LANES:(pair + 1) * LANES])
        y = (jnp.concatenate(y_tiles, axis=1) + dskip_ref[...] * xs) * _silu(z_scr[rows, :])
        return jnp.concatenate(
            [_rms(y[:, g * GROUP_W:(g + 1) * GROUP_W], norm_ref[:, g * GROUP_W:(g + 1) * GROUP_W])
             for g in range(SSD_GROUPS)], axis=1).astype(BF16)

    y_ssd = jnp.concatenate([chunk_body(c) for c in range(n_chunks)], axis=0)
    o_ref[0] = x + jnp.dot(jnp.concatenate([y_ssd] + y_pool, axis=1), wout_ref[...],
                           preferred_element_type=F32)


def _head_expand_matrix():
    m = np.zeros((2 * LANES, D_SSM), np.float32)
    for hh in range(SSD_HEADS):
        m[hh, hh * SSD_HEAD_DIM:(hh + 1) * SSD_HEAD_DIM] = 1.0
        m[LANES + hh, hh * SSD_HEAD_DIM:(hh + 1) * SSD_HEAD_DIM] = 1.0
    return jnp.asarray(m, BF16)


def _even_mixer(x3, gain, in_proj, conv_w, conv_b, dt_bias, a_log, d_skip, ssd_norm,
                pool_w, pool_scale, out_proj):
    b, t, d = x3.shape
    tm = TM_EVEN
    assert t % tm == 0 and tm % CHUNK == 0
    o_dt = D_SSM + CONV_CH
    o_pool = o_dt + SSD_HEADS
    win = jnp.concatenate(
        [in_proj[:, :o_dt], in_proj[:, o_pool:], in_proj[:, o_dt:o_pool],
         jnp.zeros((d, LANES - SSD_HEADS), in_proj.dtype)], axis=1).astype(BF16)
    n_in = win.shape[1]
    pad = LANES - SSD_HEADS
    dtb = jnp.pad(dt_bias, (0, pad)).reshape(1, LANES)
    alog = jnp.pad(a_log, (0, pad)).reshape(1, LANES)
    dskip = jnp.repeat(d_skip, SSD_HEAD_DIM).reshape(1, D_SSM)
    tril3 = jnp.asarray(np.tile(np.tril(np.ones((CHUNK, CHUNK), np.float32)), (1, 3)), BF16)
    n_g = len(POOL_WINDOWS)
    return pl.pallas_call(
        _even_kernel,
        grid=(b, t // tm),
        in_specs=[
            pl.BlockSpec((1, tm, d), lambda i, j: (i, j, 0)),
            _resident((1, d)),
            _resident((d, n_in)),
            _resident((CONV_WIDTH, CONV_CH)),
            _resident((1, CONV_CH)),
            _resident((1, LANES)),
            _resident((1, LANES)),
            _resident((1, D_SSM)),
            _resident((1, D_SSM)),
            _resident((n_g, POOL_GROUP, POOL_GROUP)),
            _resident((1, D_POOL)),
            _resident((D_SSM + D_POOL, d)),
            _resident((2 * LANES, D_SSM)),
            _resident((CHUNK, 3 * CHUNK)),
        ],
        out_specs=pl.BlockSpec((1, tm, d), lambda i, j: (i, j, 0)),
        out_shape=jax.ShapeDtypeStruct((b, t, d), F32),
        scratch_shapes=[
            pltpu.VMEM((CONV_TILES, tm + CONV_HALO, LANES), F32),
            pltpu.VMEM((tm, CONV_CH), F32),
            pltpu.VMEM((tm, LANES), F32),
            pltpu.VMEM((tm, D_SSM), F32),
            pltpu.VMEM((POOL_TILES, tm + POOL_HALO, LANES), F32),
            pltpu.VMEM((SSD_GROUPS, D_STATE, GROUP_W), F32),
            pltpu.VMEM((2, CONV_TILES, CONV_HALO, LANES), F32),
            pltpu.VMEM((2, POOL_TILES, POOL_HALO, LANES), F32),
        ],
        compiler_params=_params(2),
        name="even_mixer",
    )(x3, gain.reshape(1, d), win, conv_w, conv_b.reshape(1, CONV_CH), dtb, alog, dskip,
      ssd_norm.reshape(1, D_SSM), pool_w.astype(BF16), pool_scale.reshape(1, D_POOL),
      out_proj.astype(BF16), _head_expand_matrix(), tril3)


N_QK_HEADS = ATTN_HEADS + ATTN_KV_HEADS
QK_W = N_QK_HEADS * ATTN_HEAD_DIM
Q_W = ATTN_HEADS * ATTN_HEAD_DIM
KV_W = ATTN_KV_HEADS * ATTN_HEAD_DIM
PAIRS = ATTN_GROUP // 2
STACK = PAIRS * BLOCK
WO_BLOCKS = 4


def _odd_kernel(sink_ref, x_ref, g_ref, wqkv_ref, bqkv_ref, gqk_ref, seg_ref, exp_ref, bias_ref,
                wo_ref, bo_ref, o_ref, q_scr, k_scr, v_scr, k_prev, v_prev):
    t = pl.program_id(1)
    tm = x_ref.shape[1]
    n_blocks = tm // BLOCK
    n_forms = 2 * ATTN_KV_HEADS

    slot = lax.rem(t, 2)

    @pl.when(t == 0)
    def _():
        k_prev[0] = jnp.zeros((n_forms, BLOCK, LANES), BF16)
        v_prev[0] = jnp.zeros((n_forms, BLOCK, 2 * LANES), BF16)

    k_scr[:, 0:BLOCK, :] = k_prev[slot]
    v_scr[:, 0:BLOCK, :] = v_prev[slot]

    x = x_ref[0]
    h = _rms(x, g_ref[...]).astype(BF16)
    qkv = jnp.dot(h, wqkv_ref[...], preferred_element_type=F32) + bqkv_ref[...]
    qk = qkv[:, 0:QK_W]
    ssq = jnp.dot((qk * qk).astype(BF16), seg_ref[...], preferred_element_type=F32)
    r_hi, r_lo = _split2(lax.rsqrt(ssq * (1.0 / ATTN_HEAD_DIM) + EPS))
    r_exp = jnp.dot(jnp.concatenate([r_hi, r_lo], axis=1), exp_ref[...],
                    preferred_element_type=F32)
    qkn = qk * r_exp * gqk_ref[...]
    q_scr[...] = qkn[:, 0:Q_W].astype(BF16)

    low_half = lax.broadcasted_iota(jnp.int32, (tm, LANES), 1) < HALF
    zeros = jnp.zeros((tm, LANES), F32)
    ones_lo = jnp.where(low_half, 1.0, 0.0).astype(BF16)
    ones_hi = jnp.where(low_half, 0.0, 1.0).astype(BF16)
    for is_v, src, dst, base in ((False, qkn, k_scr, Q_W), (True, qkv, v_scr, QK_W)):
        for pair in range(ATTN_KV_HEADS // 2):
            a2 = src[:, base + pair * LANES:base + (pair + 1) * LANES]
            a2r = pltpu.roll(a2, HALF, axis=1)
            forms = (
                jnp.where(low_half, a2, zeros), jnp.where(low_half, zeros, a2r),
                jnp.where(low_half, a2r, zeros), jnp.where(low_half, zeros, a2),
            )
            for f, val in enumerate(forms):
                dst[4 * pair + f, BLOCK:BLOCK + tm, 0:LANES] = val.astype(BF16)
                if is_v:
                    dst[4 * pair + f, BLOCK:BLOCK + tm, LANES:2 * LANES] = (
                        ones_lo if f % 2 == 0 else ones_hi)

    nt_dims = (((1,), (1,)), ((), ()))
    row = lax.broadcasted_iota(jnp.int32, (STACK, LANES), 0)
    col = lax.broadcasted_iota(jnp.int32, (STACK, LANES), 1)
    tri = col <= (row & (BLOCK - 1))
    lane_low = col < HALF
    row_pair0 = lax.broadcasted_iota(jnp.int32, (STACK, 1), 0) < BLOCK

    def block_attention(i):
        rows = pl.ds(i * BLOCK, BLOCK)
        krows = pl.ds(i * BLOCK, 2 * BLOCK)
        sel = jnp.where(t == 0, 1, 0) if i == 0 else 0
        att_tiles = []
        for kvh in range(ATTN_KV_HEADS):
            col0 = kvh * ATTN_GROUP * ATTN_HEAD_DIM
            q2 = jnp.concatenate(
                [q_scr[rows, col0 + p * LANES:col0 + (p + 1) * LANES] for p in range(PAIRS)],
                axis=0)
            acc = None
            shifts = []
            for parity in range(2):
                s2 = lax.dot_general(q2, k_scr[2 * kvh + parity, krows, :], nt_dims,
                                     preferred_element_type=F32)
                sc = jnp.where(tri, s2[:, BLOCK:], s2[:, :BLOCK]) + bias_ref[sel, kvh, parity]
                sink = jnp.where(row_pair0, sink_ref[kvh * ATTN_GROUP + parity],
                                 sink_ref[kvh * ATTN_GROUP + 2 + parity])
                m = jnp.maximum(jnp.max(sc, axis=-1, keepdims=True), sink)
                p = jnp.exp(sc - m)
                pcat = jnp.concatenate([jnp.where(tri, 0.0, p), jnp.where(tri, p, 0.0)],
                                       axis=1).astype(BF16)
                part = jnp.dot(pcat, v_scr[2 * kvh + parity, krows, :],
                               preferred_element_type=F32)
                acc = part if acc is None else acc + part
                shifts.append(sink - m)
            denom = acc[:, LANES:] + jnp.exp(jnp.where(lane_low, shifts[0], shifts[1]))
            att = (acc[:, :LANES] * (1.0 / denom)).astype(BF16)
            att_tiles.extend(att[p * BLOCK:(p + 1) * BLOCK] for p in range(PAIRS))
        return jnp.concatenate(att_tiles, axis=1)

    for i0 in range(0, n_blocks, WO_BLOCKS):
        att = jnp.concatenate([block_attention(i) for i in range(i0, i0 + WO_BLOCKS)], axis=0)
        rows = pl.ds(i0 * BLOCK, WO_BLOCKS * BLOCK)
        o_ref[0, rows, :] = (x_ref[0, rows, :] + bo_ref[...]
                             + jnp.dot(att, wo_ref[...], preferred_element_type=F32))
    k_prev[1 - slot] = k_scr[:, tm:tm + BLOCK, :]
    v_prev[1 - slot] = v_scr[:, tm:tm + BLOCK, :]


def _alibi_bias():
    slopes = 2.0 ** (-8.0 * (np.arange(ATTN_HEADS) + 1) / ATTN_HEADS)
    q = np.arange(BLOCK)[:, None]
    j = np.arange(BLOCK)[None, :]
    cur = j <= q
    dist = np.where(cur, q - j, q - j + BLOCK).astype(np.float32)
    assert dist.min() >= 0 and dist.max() < WINDOW
    out = np.zeros((2, ATTN_KV_HEADS, 2, STACK, BLOCK), np.float32)
    for kvh in range(ATTN_KV_HEADS):
        for parity in range(2):
            for p in range(PAIRS):
                head = kvh * ATTN_GROUP + 2 * p + parity
                base = -np.float32(slopes[head]) * dist
                out[0, kvh, parity, p * BLOCK:(p + 1) * BLOCK] = base
                out[1, kvh, parity, p * BLOCK:(p + 1) * BLOCK] = np.where(cur, base, -np.inf)
    return jnp.asarray(out)


def _segment_matrices():
    seg = np.zeros((QK_W, LANES), np.float32)
    exp = np.zeros((2 * LANES, QK_W), np.float32)
    for hh in range(N_QK_HEADS):
        seg[hh * ATTN_HEAD_DIM:(hh + 1) * ATTN_HEAD_DIM, hh] = 1.0
        exp[hh, hh * ATTN_HEAD_DIM:(hh + 1) * ATTN_HEAD_DIM] = 1.0
        exp[LANES + hh, hh * ATTN_HEAD_DIM:(hh + 1) * ATTN_HEAD_DIM] = 1.0
    return jnp.asarray(seg, BF16), jnp.asarray(exp, BF16)


def _odd_mixer(x3, gain, wqkv, bqkv, qnorm, knorm, sinks, wo, bo):
    b, t, d = x3.shape
    tm = TM_ODD
    assert t % tm == 0 and tm % BLOCK == 0
    qkv_w = wqkv.shape[1]
    gqk = jnp.concatenate([jnp.tile(qnorm, ATTN_HEADS) * (ATTN_HEAD_DIM ** -0.5),
                           jnp.tile(knorm, ATTN_KV_HEADS)]).reshape(1, QK_W)
    seg, exp = _segment_matrices()
    n_forms = 2 * ATTN_KV_HEADS
    return pl.pallas_call(
        _odd_kernel,
        grid=(b, t // tm),
        in_specs=[
            pl.BlockSpec(memory_space=pltpu.SMEM),
            pl.BlockSpec((1, tm, d), lambda i, j: (i, j, 0)),
            _resident((1, d)),
            _resident((d, qkv_w)),
            _resident((1, qkv_w)),
            _resident((1, QK_W)),
            _resident((QK_W, LANES)),
            _resident((2 * LANES, QK_W)),
            _resident((2, ATTN_KV_HEADS, 2, STACK, BLOCK)),
            _resident((Q_W, d)),
            _resident((1, d)),
        ],
        out_specs=pl.BlockSpec((1, tm, d), lambda i, j: (i, j, 0)),
        out_shape=jax.ShapeDtypeStruct((b, t, d), F32),
        scratch_shapes=[
            pltpu.VMEM((tm, Q_W), BF16),
            pltpu.VMEM((n_forms, tm + BLOCK, LANES), BF16),
            pltpu.VMEM((n_forms, tm + BLOCK, 2 * LANES), BF16),
            pltpu.VMEM((2, n_forms, BLOCK, LANES), BF16),
            pltpu.VMEM((2, n_forms, BLOCK, 2 * LANES), BF16),
        ],
        compiler_params=_params(2),
        name="odd_mixer",
    )(sinks, x3, gain.reshape(1, d), wqkv.astype(BF16), bqkv.reshape(1, qkv_w), gqk, seg, exp,
      _alibi_bias(), wo.astype(BF16), bo.reshape(1, d))


def kernel(x, mem, mem_norm, mem_wkv, mem_knorm, ffn1_norm, ffn1_wi, ffn1_wo, mix_norm, ssd_in_proj, ssd_conv_w, ssd_conv_b, ssd_dt_bias, ssd_a_log, ssd_d, ssd_norm, pool_w, pool_scale, even_out_proj, attn_wqkv, attn_bqkv, attn_qnorm, attn_knorm, attn_sinks, attn_wo, attn_bo, xattn_norm, xattn_wq, xattn_qnorm, xattn_wo, ffn2_norm, ffn2_wi, ffn2_wo):
    b, t, d = x.shape
    depth = ffn1_norm.shape[0]
    mem_kt, mem_v = _mem_kv(mem, mem_norm, mem_wkv, mem_knorm)
    wi1, wo1 = ffn1_wi.astype(BF16), ffn1_wo.astype(BF16)
    wi2, wo2 = ffn2_wi.astype(BF16), ffn2_wo.astype(BF16)
    xq, xo = xattn_wq.astype(BF16), xattn_wo.astype(BF16)
    for i in range(depth):
        x = _ffn(x.reshape(b * t, d), ffn1_norm[i], wi1, wo1, i).reshape(b, t, d)
        if i % 2 == 0:
            e = i // 2
            x = _even_mixer(x, mix_norm[i], ssd_in_proj[e], ssd_conv_w[e], ssd_conv_b[e],
                            ssd_dt_bias[e], ssd_a_log[e], ssd_d[e], ssd_norm[e], pool_w[e],
                            pool_scale[e], even_out_proj[e])
        else:
            o = i // 2
            x = _odd_mixer(x, mix_norm[i], attn_wqkv[o], attn_bqkv[o], attn_qnorm[o], attn_knorm[o],
                           attn_sinks[o], attn_wo[o], attn_bo[o])
        x = _xattn(x, xattn_norm[i], xq, xattn_qnorm[i], mem_kt, mem_v, xo, i)
        x = _ffn(x.reshape(b * t, d), ffn2_norm[i], wi2, wo2, i).reshape(b, t, d)
    return x
```

```python
import numpy as np
import jax
import jax.numpy as jnp
from jax import lax
from jax.experimental import pallas as pl
from jax.experimental.pallas import tpu as pltpu

F32 = jnp.float32
BF16 = jnp.bfloat16

EPS = 1e-6
FFN_RESIDUAL = 0.5
SSD_HEADS = 16
SSD_HEAD_DIM = 64
SSD_GROUPS = 2
D_STATE = 128
CONV_WIDTH = 4
CHUNK = 128
D_SSM = SSD_HEADS * SSD_HEAD_DIM
GROUP_W = D_SSM // SSD_GROUPS
CONV_CH = D_SSM + 2 * SSD_GROUPS * D_STATE
POOL_WINDOWS = (2, 4, 8, 16)
POOL_GROUP = 256
D_POOL = POOL_GROUP * len(POOL_WINDOWS)
POOL_HALO = 16
CONV_HALO = 8
ATTN_HEADS = 16
ATTN_KV_HEADS = 4
ATTN_GROUP = ATTN_HEADS // ATTN_KV_HEADS
ATTN_HEAD_DIM = 64
WINDOW = 128
BLOCK = 128
MEM_HEADS = 4

LANES = 128
SUBLANES = 8
HALF = 64

TM_FFN = 1024
TM_XATTN = 1024
TM_EVEN = 512
TM_ODD = 1024
VMEM_LIMIT_BYTES = 56 * 1024 * 1024
FFN_CHUNKS = (1024, 1024, 768)


def _rms(x, gain):
    ms = jnp.mean(x * x, axis=-1, keepdims=True)
    return x * lax.rsqrt(ms + EPS) * gain


def _silu(x):
    return x * (1.0 / (1.0 + jnp.exp(-x)))


def _split2(x):
    hi = x.astype(BF16)
    lo = (x - hi.astype(F32)).astype(BF16)
    return hi, lo


def _resident(shape):
    nd = len(shape)
    return pl.BlockSpec(shape, lambda *_: (0,) * nd, pipeline_mode=pl.Buffered(1))


def _resident_layer(shape, layer):
    nd = len(shape)
    return pl.BlockSpec((None,) + tuple(shape), lambda *_: (layer,) + (0,) * nd,
                        pipeline_mode=pl.Buffered(1))


def _params(n_axes):
    return pltpu.CompilerParams(
        dimension_semantics=("arbitrary",) * n_axes,
        vmem_limit_bytes=VMEM_LIMIT_BYTES,
    )


def _memkv_kernel(mem_ref, g_ref, wkv_ref, kn_ref, kt_ref, v_ref):
    d = mem_ref.shape[-1]
    hd = d // MEM_HEADS
    h = _rms(mem_ref[0], g_ref[...]).astype(BF16)
    kv = jnp.dot(h, wkv_ref[...], preferred_element_type=F32)
    for i in range(MEM_HEADS):
        kh = _rms(kv[:, i * hd:(i + 1) * hd], kn_ref[...])
        kt_ref[0, i] = kh.T.astype(BF16)
    v_ref[0] = kv[:, d:].astype(BF16)


def _mem_kv(mem, mem_norm, wkv, mem_knorm):
    b, m, d = mem.shape
    hd = d // MEM_HEADS
    return pl.pallas_call(
        _memkv_kernel,
        grid=(b,),
        in_specs=[
            pl.BlockSpec((1, m, d), lambda i: (i, 0, 0)),
            _resident((1, d)),
            _resident((d, 2 * d)),
            _resident((1, hd)),
        ],
        out_specs=[
            pl.BlockSpec((1, MEM_HEADS, hd, m), lambda i: (i, 0, 0, 0)),
            pl.BlockSpec((1, m, d), lambda i: (i, 0, 0)),
        ],
        out_shape=[
            jax.ShapeDtypeStruct((b, MEM_HEADS, hd, m), BF16),
            jax.ShapeDtypeStruct((b, m, d), BF16),
        ],
        compiler_params=_params(1),
        name="mem_kv",
    )(mem, mem_norm.reshape(1, d), wkv.astype(BF16), mem_knorm.reshape(1, hd))


def _ffn_kernel(x_ref, g_ref, wi_ref, wo_ref, o_ref, act_ref):
    d_ff = wo_ref.shape[0]
    h = _rms(x_ref[...], g_ref[...]).astype(BF16)
    lo = 0
    for ck in FFN_CHUNKS:
        gate = jnp.dot(h, wi_ref[:, lo:lo + ck], preferred_element_type=F32)
        up = jnp.dot(h, wi_ref[:, d_ff + lo:d_ff + lo + ck], preferred_element_type=F32)
        act_ref[:, lo:lo + ck] = (_silu(gate) * up).astype(BF16)
        lo += ck
    y = jnp.dot(act_ref[...], wo_ref[...], preferred_element_type=F32)
    o_ref[...] = x_ref[...] + FFN_RESIDUAL * y


def _ffn(x2, gain, wi_all, wo_all, layer):
    n, d = x2.shape
    d_ff = wo_all.shape[1]
    assert sum(FFN_CHUNKS) == d_ff and n % TM_FFN == 0
    return pl.pallas_call(
        _ffn_kernel,
        grid=(n // TM_FFN,),
        in_specs=[
            pl.BlockSpec((TM_FFN, d), lambda i: (i, 0)),
            _resident((1, d)),
            _resident_layer((d, 2 * d_ff), layer),
            _resident_layer((d_ff, d), layer),
        ],
        out_specs=pl.BlockSpec((TM_FFN, d), lambda i: (i, 0)),
        out_shape=jax.ShapeDtypeStruct((n, d), F32),
        scratch_shapes=[pltpu.VMEM((TM_FFN, d_ff), BF16)],
        compiler_params=_params(1),
        name="ffn",
    )(x2, gain.reshape(1, d), wi_all, wo_all)


def _xattn_kernel(x_ref, g_ref, wq_ref, qn_ref, kt_ref, v_ref, wo_ref, o_ref, att_ref):
    d = x_ref.shape[-1]
    hd = d // MEM_HEADS
    h = _rms(x_ref[0], g_ref[...]).astype(BF16)
    q = jnp.dot(h, wq_ref[...], preferred_element_type=F32)
    for i in range(MEM_HEADS):
        cols = slice(i * hd, (i + 1) * hd)
        qh = (_rms(q[:, cols], qn_ref[...]) * (hd ** -0.5)).astype(BF16)
        s = jnp.dot(qh, kt_ref[0, i], preferred_element_type=F32)
        p = jnp.exp(s - jnp.max(s, axis=-1, keepdims=True))
        inv = 1.0 / jnp.sum(p, axis=-1, keepdims=True)
        o = jnp.dot(p.astype(BF16), v_ref[0, :, cols], preferred_element_type=F32)
        att_ref[:, cols] = (o * inv).astype(BF16)
    y = jnp.dot(att_ref[...], wo_ref[...], preferred_element_type=F32)
    o_ref[0] = x_ref[0] + y


def _xattn(x3, gain, wq_all, qnorm, mem_kt, mem_v, wo_all, layer):
    b, t, d = x3.shape
    hd = d // MEM_HEADS
    m = mem_v.shape[1]
    tm = TM_XATTN
    assert t % tm == 0
    return pl.pallas_call(
        _xattn_kernel,
        grid=(b, t // tm),
        in_specs=[
            pl.BlockSpec((1, tm, d), lambda i, j: (i, j, 0)),
            _resident((1, d)),
            _resident_layer((d, d), layer),
            _resident((1, hd)),
            pl.BlockSpec((1, MEM_HEADS, hd, m), lambda i, j: (i, 0, 0, 0)),
            pl.BlockSpec((1, m, d), lambda i, j: (i, 0, 0)),
            _resident_layer((d, d), layer),
        ],
        out_specs=pl.BlockSpec((1, tm, d), lambda i, j: (i, j, 0)),
        out_shape=jax.ShapeDtypeStruct((b, t, d), F32),
        scratch_shapes=[pltpu.VMEM((tm, d), BF16)],
        compiler_params=_params(2),
        name="xattn",
    )(x3, gain.reshape(1, d), wq_all, qnorm.reshape(1, hd), mem_kt, mem_v, wo_all)


CONV_TILES = CONV_CH // LANES
POOL_TILES = D_POOL // LANES
TILES_PER_POOL_GROUP = POOL_GROUP // LANES


def _even_kernel(x_ref, g_ref, win_ref, cw_ref, cb_ref, dtb_ref, alog_ref, dskip_ref, norm_ref,
                 pw_ref, ps_ref, wout_ref, exph_ref, tril_ref, o_ref,
                 conv_scr, act_scr, dt_scr, z_scr, pool_scr, state_scr, conv_halo, pool_halo):
    t = pl.program_id(1)
    tm = x_ref.shape[1]
    n_chunks = tm // CHUNK

    slot = lax.rem(t, 2)

    @pl.when(t == 0)
    def _():
        conv_halo[0] = jnp.zeros((CONV_TILES, CONV_HALO, LANES), F32)
        pool_halo[0] = jnp.zeros((POOL_TILES, POOL_HALO, LANES), F32)
        state_scr[...] = jnp.zeros(state_scr.shape, F32)

    conv_scr[:, 0:CONV_HALO, :] = conv_halo[slot]
    pool_scr[:, 0:POOL_HALO, :] = pool_halo[slot]

    x = x_ref[0]
    h = _rms(x, g_ref[...]).astype(BF16)
    o1 = D_SSM + CONV_CH
    o2 = o1 + D_POOL
    xbc = jnp.dot(h, win_ref[:, D_SSM:o1], preferred_element_type=F32)
    for c in range(CONV_TILES):
        conv_scr[c, CONV_HALO:CONV_HALO + tm, :] = xbc[:, c * LANES:(c + 1) * LANES]
    u_pool = jnp.dot(h, win_ref[:, o1:o2], preferred_element_type=F32)
    for c in range(POOL_TILES):
        pool_scr[c, POOL_HALO:POOL_HALO + tm, :] = u_pool[:, c * LANES:(c + 1) * LANES]
    dt_raw = jnp.dot(h, win_ref[:, o2:o2 + LANES], preferred_element_type=F32) + dtb_ref[...]
    dt_scr[...] = jnp.maximum(dt_raw, 0.0) + jnp.log1p(jnp.exp(-jnp.abs(dt_raw)))
    z_scr[...] = jnp.dot(h, win_ref[:, 0:D_SSM], preferred_element_type=F32)

    for c in range(CONV_TILES):
        cols = slice(c * LANES, (c + 1) * LANES)
        acc = cb_ref[:, cols]
        for k in range(CONV_WIDTH):
            acc = acc + (conv_scr[c, pl.ds(CONV_HALO - CONV_WIDTH + 1 + k, tm), :]
                         * cw_ref[k:k + 1, cols])
        act_scr[:, cols] = _silu(acc)
    conv_halo[1 - slot] = conv_scr[:, tm:tm + CONV_HALO, :]

    pos = t * tm + lax.broadcasted_iota(jnp.int32, (tm, 1), 0)
    y_pool = []
    for k, w in enumerate(POOL_WINDOWS):
        inv_count = 1.0 / jnp.minimum(pos + 1, w).astype(F32)
        pooled = []
        for c in range(k * TILES_PER_POOL_GROUP, (k + 1) * TILES_PER_POOL_GROUP):
            if w > SUBLANES:
                e = pool_scr[c, pl.ds(POOL_HALO - SUBLANES, tm + SUBLANES), :]
                for i in range(1, SUBLANES):
                    e = e + pool_scr[c, pl.ds(POOL_HALO - SUBLANES - i, tm + SUBLANES), :]
                s = e[SUBLANES:, :] + e[:tm, :]
            else:
                s = pool_scr[c, pl.ds(POOL_HALO, tm), :]
                for i in range(1, w):
                    s = s + pool_scr[c, pl.ds(POOL_HALO - i, tm), :]
            pooled.append(s * inv_count - pool_scr[c, pl.ds(POOL_HALO, tm), :])
        pooled = jnp.concatenate(pooled, axis=1).astype(BF16)
        kcols = slice(k * POOL_GROUP, (k + 1) * POOL_GROUP)
        yk = jnp.dot(pooled, pw_ref[k], preferred_element_type=F32)
        y_pool.append((yk * ps_ref[:, kcols]).astype(BF16))
    pool_halo[1 - slot] = pool_scr[:, tm:tm + POOL_HALO, :]

    a_row = -jnp.exp(alog_ref[...])
    low_half = (lax.broadcasted_iota(jnp.int32, (CHUNK, D_SSM), 1) & HALF) == 0
    causal = (lax.broadcasted_iota(jnp.int32, (CHUNK, CHUNK), 0)
              >= lax.broadcasted_iota(jnp.int32, (CHUNK, CHUNK), 1))
    heads_per_group = SSD_HEADS // SSD_GROUPS

    def chunk_body(c):
        rows = pl.ds(c * CHUNK, CHUNK)
        xs = act_scr[rows, 0:D_SSM]
        dt = dt_scr[rows, :]
        adt = dt * a_row
        hi = adt.astype(BF16)
        r1 = adt - hi.astype(F32)
        mid = r1.astype(BF16)
        lo = (r1 - mid.astype(F32)).astype(BF16)
        acs = jnp.dot(tril_ref[...], jnp.concatenate([hi, mid, lo], axis=0),
                      preferred_element_type=F32)
        acs_last = acs[CHUNK - 1:CHUNK, :]
        stacked = jnp.concatenate([dt, jnp.exp(acs_last - acs), jnp.exp(acs)], axis=0)
        s_hi, s_lo = _split2(stacked)
        expanded = jnp.dot(jnp.concatenate([s_hi, s_lo], axis=1), exph_ref[...],
                           preferred_element_type=F32)
        dt_exp = expanded[0:CHUNK]
        dte_exp = expanded[CHUNK:2 * CHUNK]
        dfs_exp = expanded[2 * CHUNK:3 * CHUNK]
        xdt = xs * dt_exp
        xw_b = (xdt * dte_exp).astype(BF16)
        x_even = jnp.where(low_half, xdt, 0.0).astype(BF16)
        x_odd = jnp.where(low_half, 0.0, xdt).astype(BF16)
        acs_row = acs.T

        y_tiles = []
        for g in range(SSD_GROUPS):
            gcols = slice(g * GROUP_W, (g + 1) * GROUP_W)
            b_g = act_scr[rows, D_SSM + g * D_STATE:D_SSM + (g + 1) * D_STATE]
            c_g = act_scr[rows, D_SSM + SSD_GROUPS * D_STATE + g * D_STATE:
                          D_SSM + SSD_GROUPS * D_STATE + (g + 1) * D_STATE].astype(BF16)
            bt_g = b_g.T.astype(BF16)
            cb = jnp.dot(c_g, bt_g, preferred_element_type=F32)
            s_new = jnp.dot(bt_g, xw_b[:, gcols], preferred_element_type=F32)
            s_in = state_scr[g]
            y_off = jnp.dot(c_g, s_in.astype(BF16), preferred_element_type=F32) * dfs_exp[:, gcols]
            state_scr[g] = s_in * dfs_exp[CHUNK - 1:CHUNK, gcols] + s_new
            for pair in range(heads_per_group // 2):
                pcols = slice(g * GROUP_W + pair * LANES, g * GROUP_W + (pair + 1) * LANES)
                w2 = []
                for parity in range(2):
                    hh = g * heads_per_group + 2 * pair + parity
                    seg = jnp.where(causal, jnp.exp(acs[:, hh:hh + 1] - acs_row[hh:hh + 1, :]), 0.0)
                    w2.append((cb * seg).astype(BF16))
                y_diag = jnp.dot(jnp.concatenate(w2, axis=1),
                                 jnp.concatenate([x_even[:, pcols], x_odd[:, pcols]], axis=0),
                                 preferred_element_type=F32)
                y_tiles.append(y_diag + y_off[:, pair * LANES:(pair + 1) * LANES])
        y = (jnp.concatenate(y_tiles, axis=1) + dskip_ref[...] * xs) * _silu(z_scr[rows, :])
        return jnp.concatenate(
            [_rms(y[:, g * GROUP_W:(g + 1) * GROUP_W], norm_ref[:, g * GROUP_W:(g + 1) * GROUP_W])
             for g in range(SSD_GROUPS)], axis=1).astype(BF16)

    y_ssd = jnp.concatenate([chunk_body(c) for c in range(n_chunks)], axis=0)
    o_ref[0] = x + jnp.dot(jnp.concatenate([y_ssd] + y_pool, axis=1), wout_ref[...],
                           preferred_element_type=F32)


def _head_expand_matrix():
    m = np.zeros((2 * LANES, D_SSM), np.float32)
    for hh in range(SSD_HEADS):
        m[hh, hh * SSD_HEAD_DIM:(hh + 1) * SSD_HEAD_DIM] = 1.0
        m[LANES + hh, hh * SSD_HEAD_DIM:(hh + 1) * SSD_HEAD_DIM] = 1.0
    return jnp.asarray(m, BF16)


def _even_mixer(x3, gain, in_proj, conv_w, conv_b, dt_bias, a_log, d_skip, ssd_norm,
                pool_w, pool_scale, out_proj):
    b, t, d = x3.shape
    tm = TM_EVEN
    assert t % tm == 0 and tm % CHUNK == 0
    o_dt = D_SSM + CONV_CH
    o_pool = o_dt + SSD_HEADS
    win = jnp.concatenate(
        [in_proj[:, :o_dt], in_proj[:, o_pool:], in_proj[:, o_dt:o_pool],
         jnp.zeros((d, LANES - SSD_HEADS), in_proj.dtype)], axis=1).astype(BF16)
    n_in = win.shape[1]
    pad = LANES - SSD_HEADS
    dtb = jnp.pad(dt_bias, (0, pad)).reshape(1, LANES)
    alog = jnp.pad(a_log, (0, pad)).reshape(1, LANES)
    dskip = jnp.repeat(d_skip, SSD_HEAD_DIM).reshape(1, D_SSM)
    tril3 = jnp.asarray(np.tile(np.tril(np.ones((CHUNK, CHUNK), np.float32)), (1, 3)), BF16)
    n_g = len(POOL_WINDOWS)
    return pl.pallas_call(
        _even_kernel,
        grid=(b, t // tm),
        in_specs=[
            pl.BlockSpec((1, tm, d), lambda i, j: (i, j, 0)),
            _resident((1, d)),
            _resident((d, n_in)),
            _resident((CONV_WIDTH, CONV_CH)),
            _resident((1, CONV_CH)),
            _resident((1, LANES)),
            _resident((1, LANES)),
            _resident((1, D_SSM)),
            _resident((1, D_SSM)),
            _resident((n_g, POOL_GROUP, POOL_GROUP)),
            _resident((1, D_POOL)),
            _resident((D_SSM + D_POOL, d)),
            _resident((2 * LANES, D_SSM)),
            _resident((CHUNK, 3 * CHUNK)),
        ],
        out_specs=pl.BlockSpec((1, tm, d), lambda i, j: (i, j, 0)),
        out_shape=jax.ShapeDtypeStruct((b, t, d), F32),
        scratch_shapes=[
            pltpu.VMEM((CONV_TILES, tm + CONV_HALO, LANES), F32),
            pltpu.VMEM((tm, CONV_CH), F32),
            pltpu.VMEM((tm, LANES), F32),
            pltpu.VMEM((tm, D_SSM), F32),
            pltpu.VMEM((POOL_TILES, tm + POOL_HALO, LANES), F32),
            pltpu.VMEM((SSD_GROUPS, D_STATE, GROUP_W), F32),
            pltpu.VMEM((2, CONV_TILES, CONV_HALO, LANES), F32),
            pltpu.VMEM((2, POOL_TILES, POOL_HALO, LANES), F32),
        ],
        compiler_params=_params(2),
        name="even_mixer",
    )(x3, gain.reshape(1, d), win, conv_w, conv_b.reshape(1, CONV_CH), dtb, alog, dskip,
      ssd_norm.reshape(1, D_SSM), pool_w.astype(BF16), pool_scale.reshape(1, D_POOL),
      out_proj.astype(BF16), _head_expand_matrix(), tril3)


N_QK_HEADS = ATTN_HEADS + ATTN_KV_HEADS
QK_W = N_QK_HEADS * ATTN_HEAD_DIM
Q_W = ATTN_HEADS * ATTN_HEAD_DIM
KV_W = ATTN_KV_HEADS * ATTN_HEAD_DIM
PAIRS = ATTN_GROUP // 2
STACK = PAIRS * BLOCK
WO_BLOCKS = 4


def _odd_kernel(sink_ref, x_ref, g_ref, wqkv_ref, bqkv_ref, gqk_ref, seg_ref, exp_ref, bias_ref,
                wo_ref, bo_ref, o_ref, q_scr, k_scr, v_scr, k_prev, v_prev):
    t = pl.program_id(1)
    tm = x_ref.shape[1]
    n_blocks = tm // BLOCK
    n_forms = 2 * ATTN_KV_HEADS

    slot = lax.rem(t, 2)

    @pl.when(t == 0)
    def _():
        k_prev[0] = jnp.zeros((n_forms, BLOCK, LANES), BF16)
        v_prev[0] = jnp.zeros((n_forms, BLOCK, 2 * LANES), BF16)

    k_scr[:, 0:BLOCK, :] = k_prev[slot]
    v_scr[:, 0:BLOCK, :] = v_prev[slot]

    x = x_ref[0]
    h = _rms(x, g_ref[...]).astype(BF16)
    qkv = jnp.dot(h, wqkv_ref[...], preferred_element_type=F32) + bqkv_ref[...]
    qk = qkv[:, 0:QK_W]
    ssq = jnp.dot((qk * qk).astype(BF16), seg_ref[...], preferred_element_type=F32)
    r_hi, r_lo = _split2(lax.rsqrt(ssq * (1.0 / ATTN_HEAD_DIM) + EPS))
    r_exp = jnp.dot(jnp.concatenate([r_hi, r_lo], axis=1), exp_ref[...],
                    preferred_element_type=F32)
    qkn = qk * r_exp * gqk_ref[...]
    q_scr[...] = qkn[:, 0:Q_W].astype(BF16)

    low_half = lax.broadcasted_iota(jnp.int32, (tm, LANES), 1) < HALF
    zeros = jnp.zeros((tm, LANES), F32)
    ones_lo = jnp.where(low_half, 1.0, 0.0).astype(BF16)
    ones_hi = jnp.where(low_half, 0.0, 1.0).astype(BF16)
    for is_v, src, dst, base in ((False, qkn, k_scr, Q_W), (True, qkv, v_scr, QK_W)):
        for pair in range(ATTN_KV_HEADS // 2):
            a2 = src[:, base + pair * LANES:base + (pair + 1) * LANES]
            a2r = pltpu.roll(a2, HALF, axis=1)
            forms = (
                jnp.where(low_half, a2, zeros), jnp.where(low_half, zeros, a2r),
                jnp.where(low_half, a2r, zeros), jnp.where(low_half, zeros, a2),
            )
            for f, val in enumerate(forms):
                dst[4 * pair + f, BLOCK:BLOCK + tm, 0:LANES] = val.astype(BF16)
                if is_v:
                    dst[4 * pair + f, BLOCK:BLOCK + tm, LANES:2 * LANES] = (
                        ones_lo if f % 2 == 0 else ones_hi)

    nt_dims = (((1,), (1,)), ((), ()))
    row = lax.broadcasted_iota(jnp.int32, (STACK, LANES), 0)
    col = lax.broadcasted_iota(jnp.int32, (STACK, LANES), 1)
    tri = col <= (row & (BLOCK - 1))
    lane_low = col < HALF
    row_pair0 = lax.broadcasted_iota(jnp.int32, (STACK, 1), 0) < BLOCK

    def block_attention(i):
        rows = pl.ds(i * BLOCK, BLOCK)
        krows = pl.ds(i * BLOCK, 2 * BLOCK)
        sel = jnp.where(t == 0, 1, 0) if i == 0 else 0
        att_tiles = []
        for kvh in range(ATTN_KV_HEADS):
            col0 = kvh * ATTN_GROUP * ATTN_HEAD_DIM
            q2 = jnp.concatenate(
                [q_scr[rows, col0 + p * LANES:col0 + (p + 1) * LANES] for p in range(PAIRS)],
                axis=0)
            acc = None
            shifts = []
            for parity in range(2):
                s2 = lax.dot_general(q2, k_scr[2 * kvh + parity, krows, :], nt_dims,
                                     preferred_element_type=F32)
                sc = jnp.where(tri, s2[:, BLOCK:], s2[:, :BLOCK]) + bias_ref[sel, kvh, parity]
                sink = jnp.where(row_pair0, sink_ref[kvh * ATTN_GROUP + parity],
                                 sink_ref[kvh * ATTN_GROUP + 2 + parity])
                m = jnp.maximum(jnp.max(sc, axis=-1, keepdims=True), sink)
                p = jnp.exp(sc - m)
                pcat = jnp.concatenate([jnp.where(tri, 0.0, p), jnp.where(tri, p, 0.0)],
                                       axis=1).astype(BF16)
                part = jnp.dot(pcat, v_scr[2 * kvh + parity, krows, :],
                               preferred_element_type=F32)
                acc = part if acc is None else acc + part
                shifts.append(sink - m)
            denom = acc[:, LANES:] + jnp.exp(jnp.where(lane_low, shifts[0], shifts[1]))
            att = (acc[:, :LANES] * (1.0 / denom)).astype(BF16)
            att_tiles.extend(att[p * BLOCK:(p + 1) * BLOCK] for p in range(PAIRS))
        return jnp.concatenate(att_tiles, axis=1)

    for i0 in range(0, n_blocks, WO_BLOCKS):
        att = jnp.concatenate([block_attention(i) for i in range(i0, i0 + WO_BLOCKS)], axis=0)
        rows = pl.ds(i0 * BLOCK, WO_BLOCKS * BLOCK)
        o_ref[0, rows, :] = (x_ref[0, rows, :] + bo_ref[...]
                             + jnp.dot(att, wo_ref[...], preferred_element_type=F32))
    k_prev[1 - slot] = k_scr[:, tm:tm + BLOCK, :]
    v_prev[1 - slot] = v_scr[:, tm:tm + BLOCK, :]


def _alibi_bias():
    slopes = 2.0 ** (-8.0 * (np.arange(ATTN_HEADS) + 1) / ATTN_HEADS)
    q = np.arange(BLOCK)[:, None]
    j = np.arange(BLOCK)[None, :]
    cur = j <= q
    dist = np.where(cur, q - j, q - j + BLOCK).astype(np.float32)
    assert dist.min() >= 0 and dist.max() < WINDOW
    out = np.zeros((2, ATTN_KV_HEADS, 2, STACK, BLOCK), np.float32)
    for kvh in range(ATTN_KV_HEADS):
        for parity in range(2):
            for p in range(PAIRS):
                head = kvh * ATTN_GROUP + 2 * p + parity
                base = -np.float32(slopes[head]) * dist
                out[0, kvh, parity, p * BLOCK:(p + 1) * BLOCK] = base
                out[1, kvh, parity, p * BLOCK:(p + 1) * BLOCK] = np.where(cur, base, -np.inf)
    return jnp.asarray(out)


def _segment_matrices():
    seg = np.zeros((QK_W, LANES), np.float32)
    exp = np.zeros((2 * LANES, QK_W), np.float32)
    for hh in range(N_QK_HEADS):
        seg[hh * ATTN_HEAD_DIM:(hh + 1) * ATTN_HEAD_DIM, hh] = 1.0
        exp[hh, hh * ATTN_HEAD_DIM:(hh + 1) * ATTN_HEAD_DIM] = 1.0
        exp[LANES + hh, hh * ATTN_HEAD_DIM:(hh + 1) * ATTN_HEAD_DIM] = 1.0
    return jnp.asarray(seg, BF16), jnp.asarray(exp, BF16)


def _odd_mixer(x3, gain, wqkv, bqkv, qnorm, knorm, sinks, wo, bo):
    b, t, d = x3.shape
    tm = TM_ODD
    assert t % tm == 0 and tm % BLOCK == 0
    qkv_w = wqkv.shape[1]
    gqk = jnp.concatenate([jnp.tile(qnorm, ATTN_HEADS) * (ATTN_HEAD_DIM ** -0.5),
                           jnp.tile(knorm, ATTN_KV_HEADS)]).reshape(1, QK_W)
    seg, exp = _segment_matrices()
    n_forms = 2 * ATTN_KV_HEADS
    return pl.pallas_call(
        _odd_kernel,
        grid=(b, t // tm),
        in_specs=[
            pl.BlockSpec(memory_space=pltpu.SMEM),
            pl.BlockSpec((1, tm, d), lambda i, j: (i, j, 0)),
            _resident((1, d)),
            _resident((d, qkv_w)),
            _resident((1, qkv_w)),
            _resident((1, QK_W)),
            _resident((QK_W, LANES)),
            _resident((2 * LANES, QK_W)),
            _resident((2, ATTN_KV_HEADS, 2, STACK, BLOCK)),
            _resident((Q_W, d)),
            _resident((1, d)),
        ],
        out_specs=pl.BlockSpec((1, tm, d), lambda i, j: (i, j, 0)),
        out_shape=jax.ShapeDtypeStruct((b, t, d), F32),
        scratch_shapes=[
            pltpu.VMEM((tm, Q_W), BF16),
            pltpu.VMEM((n_forms, tm + BLOCK, LANES), BF16),
            pltpu.VMEM((n_forms, tm + BLOCK, 2 * LANES), BF16),
            pltpu.VMEM((2, n_forms, BLOCK, LANES), BF16),
            pltpu.VMEM((2, n_forms, BLOCK, 2 * LANES), BF16),
        ],
        compiler_params=_params(2),
        name="odd_mixer",
    )(sinks, x3, gain.reshape(1, d), wqkv.astype(BF16), bqkv.reshape(1, qkv_w), gqk, seg, exp,
      _alibi_bias(), wo.astype(BF16), bo.reshape(1, d))


def kernel(x, mem, mem_norm, mem_wkv, mem_knorm, ffn1_norm, ffn1_wi, ffn1_wo, mix_norm, ssd_in_proj, ssd_conv_w, ssd_conv_b, ssd_dt_bias, ssd_a_log, ssd_d, ssd_norm, pool_w, pool_scale, even_out_proj, attn_wqkv, attn_bqkv, attn_qnorm, attn_knorm, attn_sinks, attn_wo, attn_bo, xattn_norm, xattn_wq, xattn_qnorm, xattn_wo, ffn2_norm, ffn2_wi, ffn2_wo):
    b, t, d = x.shape
    depth = ffn1_norm.shape[0]
    mem_kt, mem_v = _mem_kv(mem, mem_norm, mem_wkv, mem_knorm)
    wi1, wo1 = ffn1_wi.astype(BF16), ffn1_wo.astype(BF16)
    wi2, wo2 = ffn2_wi.astype(BF16), ffn2_wo.astype(BF16)
    xq, xo = xattn_wq.astype(BF16), xattn_wo.astype(BF16)
    for i in range(depth):
        x = _ffn(x.reshape(b * t, d), ffn1_norm[i], wi1, wo1, i).reshape(b, t, d)
        if i % 2 == 0:
            e = i // 2
            x = _even_mixer(x, mix_norm[i], ssd_in_proj[e], ssd_conv_w[e], ssd_conv_b[e],
                            ssd_dt_bias[e], ssd_a_log[e], ssd_d[e], ssd_norm[e], pool_w[e],
                            pool_scale[e], even_out_proj[e])
        else:
            o = i // 2
            x = _odd_mixer(x, mix_norm[i], attn_wqkv[o], attn_bqkv[o], attn_qnorm[o], attn_knorm[o],
                           attn_sinks[o], attn_wo[o], attn_bo[o])
        x = _xattn(x, xattn_norm[i], xq, xattn_qnorm[i], mem_kt, mem_v, xo, i)
        x = _ffn(x.reshape(b * t, d), ffn2_norm[i], wi2, wo2, i).reshape(b, t, d)
    return x
```

```python
import numpy as np
import jax
import jax.numpy as jnp
from jax import lax
from jax.experimental import pallas as pl
from jax.experimental.pallas import tpu as pltpu

F32 = jnp.float32
BF16 = jnp.bfloat16

EPS = 1e-6
FFN_RESIDUAL = 0.5
SSD_HEADS = 16
SSD_HEAD_DIM = 64
SSD_GROUPS = 2
D_STATE = 128
CONV_WIDTH = 4
CHUNK = 128
D_SSM = SSD_HEADS * SSD_HEAD_DIM
GROUP_W = D_SSM // SSD_GROUPS
CONV_CH = D_SSM + 2 * SSD_GROUPS * D_STATE
POOL_WINDOWS = (2, 4, 8, 16)
POOL_GROUP = 256
D_POOL = POOL_GROUP * len(POOL_WINDOWS)
POOL_HALO = 16
CONV_HALO = 8
ATTN_HEADS = 16
ATTN_KV_HEADS = 4
ATTN_GROUP = ATTN_HEADS // ATTN_KV_HEADS
ATTN_HEAD_DIM = 64
WINDOW = 128
BLOCK = 128
MEM_HEADS = 4

LANES = 128
SUBLANES = 8
HALF = 64
MXU_WIDTH = 256
BF16_ROWS = 16
CAST_STEPS = 8

TM_FFN = 1024
TM_XATTN = 1024
TM_EVEN = 512
TM_ODD = 1024
VMEM_LIMIT_BYTES = 56 * 1024 * 1024
FFN_CHUNKS = (1024, 1024, 768)


def _rms(x, gain):
    ms = jnp.mean(x * x, axis=-1, keepdims=True)
    return x * lax.rsqrt(ms + EPS) * gain


def _silu(x):
    return x * (1.0 / (1.0 + jnp.exp(-x)))


def _split2(x):
    hi = x.astype(BF16)
    lo = (x - hi.astype(F32)).astype(BF16)
    return hi, lo


def _resident(shape):
    nd = len(shape)
    return pl.BlockSpec(shape, lambda *_: (0,) * nd, pipeline_mode=pl.Buffered(1))


def _params(n_axes):
    return pltpu.CompilerParams(
        dimension_semantics=("arbitrary",) * n_axes,
        vmem_limit_bytes=VMEM_LIMIT_BYTES,
    )


def _memkv_kernel(mem_ref, g_ref, wkv_ref, kn_ref, kt_ref, v_ref):
    d = mem_ref.shape[-1]
    hd = d // MEM_HEADS
    h = _rms(mem_ref[0], g_ref[...]).astype(BF16)
    kv = jnp.dot(h, wkv_ref[...], preferred_element_type=F32)
    for i in range(MEM_HEADS):
        kh = _rms(kv[:, i * hd:(i + 1) * hd], kn_ref[...])
        kt_ref[0, i] = kh.T.astype(BF16)
    v_ref[0] = kv[:, d:].astype(BF16)


def _mem_kv(mem, mem_norm, wkv, mem_knorm):
    b, m, d = mem.shape
    hd = d // MEM_HEADS
    return pl.pallas_call(
        _memkv_kernel,
        grid=(b,),
        in_specs=[
            pl.BlockSpec((1, m, d), lambda i: (i, 0, 0)),
            _resident((1, d)),
            _resident((d, 2 * d)),
            _resident((1, hd)),
        ],
        out_specs=[
            pl.BlockSpec((1, MEM_HEADS, hd, m), lambda i: (i, 0, 0, 0)),
            pl.BlockSpec((1, m, d), lambda i: (i, 0, 0)),
        ],
        out_shape=[
            jax.ShapeDtypeStruct((b, MEM_HEADS, hd, m), BF16),
            jax.ShapeDtypeStruct((b, m, d), BF16),
        ],
        compiler_params=_params(1),
        name="mem_kv",
    )(mem, mem_norm.reshape(1, d), wkv.astype(BF16), mem_knorm.reshape(1, hd))


def _cast_job_specs(jobs, n_steps):
    in_specs, out_specs, out_shapes = [], [], []
    for src, layer in jobs:
        rows, cols = src.shape[-2:]
        rb = next(r for r in range(BF16_ROWS, rows + 1, BF16_ROWS)
                  if rows % r == 0 and rows // r <= n_steps)
        last = rows // rb - 1
        if layer is None:
            in_specs.append(pl.BlockSpec((rb, cols), lambda i, last=last: (jnp.minimum(i, last), 0)))
        else:
            in_specs.append(pl.BlockSpec(
                (None, rb, cols), lambda i, last=last, layer=layer: (layer, jnp.minimum(i, last), 0)))
        out_specs.append(pl.BlockSpec((rb, cols), lambda i, last=last: (jnp.minimum(i, last), 0)))
        out_shapes.append(jax.ShapeDtypeStruct((rows, cols), BF16))
    return in_specs, out_specs, out_shapes


def _run_cast_jobs(in_refs, out_refs):
    for src_ref, dst_ref in zip(in_refs, out_refs):
        dst_ref[...] = src_ref[...].astype(BF16)


def _cast_kernel(*refs):
    n = len(refs) // 2
    _run_cast_jobs(refs[:n], refs[n:])


def _cast_weights(jobs):
    in_specs, out_specs, out_shapes = _cast_job_specs(jobs, CAST_STEPS)
    return pl.pallas_call(
        _cast_kernel,
        grid=(CAST_STEPS,),
        in_specs=in_specs,
        out_specs=out_specs,
        out_shape=out_shapes,
        compiler_params=_params(1),
        name="cast_weights",
    )(*[src for src, _ in jobs])


def _ffn_kernel(x_ref, g_ref, wi_ref, wo_ref, *refs):
    n_jobs = (len(refs) - 2) // 2
    o_ref, act_ref = refs[n_jobs], refs[-1]
    _run_cast_jobs(refs[:n_jobs], refs[n_jobs + 1:2 * n_jobs + 1])
    d_ff = wo_ref.shape[0]
    h = _rms(x_ref[...], g_ref[...]).astype(BF16)
    lo = 0
    for ck in FFN_CHUNKS:
        gate = jnp.dot(h, wi_ref[:, lo:lo + ck], preferred_element_type=F32)
        up = jnp.dot(h, wi_ref[:, d_ff + lo:d_ff + lo + ck], preferred_element_type=F32)
        act_ref[:, lo:lo + ck] = (_silu(gate) * up).astype(BF16)
        lo += ck
    y = jnp.dot(act_ref[...], wo_ref[...], preferred_element_type=F32)
    o_ref[...] = x_ref[...] + FFN_RESIDUAL * y


def _ffn(x2, gain, wi, wo, cast_jobs=()):
    n, d = x2.shape
    d_ff = wo.shape[0]
    assert sum(FFN_CHUNKS) == d_ff and n % TM_FFN == 0
    n_steps = n // TM_FFN
    job_in, job_out, job_shapes = _cast_job_specs(cast_jobs, n_steps)
    outs = pl.pallas_call(
        _ffn_kernel,
        grid=(n_steps,),
        in_specs=[
            pl.BlockSpec((TM_FFN, d), lambda i: (i, 0)),
            _resident((1, d)),
            _resident((d, 2 * d_ff)),
            _resident((d_ff, d)),
        ] + job_in,
        out_specs=[pl.BlockSpec((TM_FFN, d), lambda i: (i, 0))] + job_out,
        out_shape=[jax.ShapeDtypeStruct((n, d), F32)] + job_shapes,
        scratch_shapes=[pltpu.VMEM((TM_FFN, d_ff), BF16)],
        compiler_params=_params(1),
        name="ffn",
    )(x2, gain.reshape(1, d), wi, wo, *[src for src, _ in cast_jobs])
    return outs[0], outs[1:]


def _xattn_kernel(x_ref, g_ref, wq_ref, qn_ref, kt_ref, v_ref, wo_ref, o_ref, att_ref):
    d = x_ref.shape[-1]
    hd = d // MEM_HEADS
    h = _rms(x_ref[0], g_ref[...]).astype(BF16)
    q = jnp.dot(h, wq_ref[...], preferred_element_type=F32)
    for i in range(MEM_HEADS):
        cols = slice(i * hd, (i + 1) * hd)
        qh = (_rms(q[:, cols], qn_ref[...]) * (hd ** -0.5)).astype(BF16)
        s = jnp.dot(qh, kt_ref[0, i], preferred_element_type=F32)
        p = jnp.exp(s - jnp.max(s, axis=-1, keepdims=True))
        inv = 1.0 / jnp.sum(p, axis=-1, keepdims=True)
        o = jnp.dot(p.astype(BF16), v_ref[0, :, cols], preferred_element_type=F32)
        att_ref[:, cols] = (o * inv).astype(BF16)
    y = jnp.dot(att_ref[...], wo_ref[...], preferred_element_type=F32)
    o_ref[0] = x_ref[0] + y


def _xattn(x3, gain, wq, qnorm, mem_kt, mem_v, wo):
    b, t, d = x3.shape
    hd = d // MEM_HEADS
    m = mem_v.shape[1]
    tm = TM_XATTN
    assert t % tm == 0
    return pl.pallas_call(
        _xattn_kernel,
        grid=(b, t // tm),
        in_specs=[
            pl.BlockSpec((1, tm, d), lambda i, j: (i, j, 0)),
            _resident((1, d)),
            _resident((d, d)),
            _resident((1, hd)),
            pl.BlockSpec((1, MEM_HEADS, hd, m), lambda i, j: (i, 0, 0, 0)),
            pl.BlockSpec((1, m, d), lambda i, j: (i, 0, 0)),
            _resident((d, d)),
        ],
        out_specs=pl.BlockSpec((1, tm, d), lambda i, j: (i, j, 0)),
        out_shape=jax.ShapeDtypeStruct((b, t, d), F32),
        scratch_shapes=[pltpu.VMEM((tm, d), BF16)],
        compiler_params=_params(2),
        name="xattn",
    )(x3, gain.reshape(1, d), wq, qnorm.reshape(1, hd), mem_kt, mem_v, wo)


CONV_TILES = CONV_CH // LANES
POOL_TILES = D_POOL // LANES
TILES_PER_POOL_GROUP = POOL_GROUP // LANES


def _even_kernel(x_ref, g_ref, win_ref, cw_ref, cb_ref, dtb_ref, alog_ref, dskip_ref, norm_ref,
                 pw_ref, ps_ref, wout_ref, exph_ref, tril_ref, o_ref,
                 conv_scr, act_scr, dt_scr, z_scr, pool_scr, state_scr, conv_halo, pool_halo):
    t = pl.program_id(1)
    tm = x_ref.shape[1]
    n_chunks = tm // CHUNK

    slot = lax.rem(t, 2)

    @pl.when(t == 0)
    def _():
        conv_halo[0] = jnp.zeros((CONV_TILES, CONV_HALO, LANES), F32)
        pool_halo[0] = jnp.zeros((POOL_TILES, POOL_HALO, LANES), F32)
        state_scr[...] = jnp.zeros(state_scr.shape, F32)

    conv_scr[:, 0:CONV_HALO, :] = conv_halo[slot]
    pool_scr[:, 0:POOL_HALO, :] = pool_halo[slot]

    x = x_ref[0]
    h = _rms(x, g_ref[...]).astype(BF16)
    o1 = D_SSM + CONV_CH
    o2 = o1 + D_POOL
    xbc = jnp.dot(h, win_ref[:, D_SSM:o1], preferred_element_type=F32)
    for c in range(CONV_TILES):
        conv_scr[c, CONV_HALO:CONV_HALO + tm, :] = xbc[:, c * LANES:(c + 1) * LANES]
    u_pool = jnp.dot(h, win_ref[:, o1:o2], preferred_element_type=F32)
    for c in range(POOL_TILES):
        pool_scr[c, POOL_HALO:POOL_HALO + tm, :] = u_pool[:, c * LANES:(c + 1) * LANES]
    dt_raw = jnp.dot(h, win_ref[:, o2:o2 + LANES], preferred_element_type=F32) + dtb_ref[...]
    dt_scr[...] = jnp.maximum(dt_raw, 0.0) + jnp.log1p(jnp.exp(-jnp.abs(dt_raw)))
    z_scr[...] = jnp.dot(h, win_ref[:, 0:D_SSM], preferred_element_type=F32)

    for c in range(CONV_TILES):
        cols = slice(c * LANES, (c + 1) * LANES)
        acc = cb_ref[:, cols]
        for k in range(CONV_WIDTH):
            acc = acc + (conv_scr[c, pl.ds(CONV_HALO - CONV_WIDTH + 1 + k, tm), :]
                         * cw_ref[k:k + 1, cols])
        act_scr[:, cols] = _silu(acc)
    conv_halo[1 - slot] = conv_scr[:, tm:tm + CONV_HALO, :]

    pos = t * tm + lax.broadcasted_iota(jnp.int32, (tm, 1), 0)
    y_pool = []
    for k, w in enumerate(POOL_WINDOWS):
        inv_count = 1.0 / jnp.minimum(pos + 1, w).astype(F32)
        pooled = []
        for c in range(k * TILES_PER_POOL_GROUP, (k + 1) * TILES_PER_POOL_GROUP):
            if w > SUBLANES:
                e = pool_scr[c, pl.ds(POOL_HALO - SUBLANES, tm + SUBLANES), :]
                for i in range(1, SUBLANES):
                    e = e + pool_scr[c, pl.ds(POOL_HALO - SUBLANES - i, tm + SUBLANES), :]
                s = e[SUBLANES:, :] + e[:tm, :]
            else:
                s = pool_scr[c, pl.ds(POOL_HALO, tm), :]
                for i in range(1, w):
                    s = s + pool_scr[c, pl.ds(POOL_HALO - i, tm), :]
            pooled.append(s * inv_count - pool_scr[c, pl.ds(POOL_HALO, tm), :])
        pooled = jnp.concatenate(pooled, axis=1).astype(BF16)
        kcols = slice(k * POOL_GROUP, (k + 1) * POOL_GROUP)
        yk = jnp.dot(pooled, pw_ref[k], preferred_element_type=F32)
        y_pool.append((yk * ps_ref[:, kcols]).astype(BF16))
    pool_halo[1 - slot] = pool_scr[:, tm:tm + POOL_HALO, :]

    a_row = -jnp.exp(alog_ref[...])
    low_half = (lax.broadcasted_iota(jnp.int32, (CHUNK, D_SSM), 1) & HALF) == 0
    causal = (lax.broadcasted_iota(jnp.int32, (CHUNK, CHUNK), 0)
              >= lax.broadcasted_iota(jnp.int32, (CHUNK, CHUNK), 1))
    heads_per_group = SSD_HEADS // SSD_GROUPS

    def chunk_body(c):
        rows = pl.ds(c * CHUNK, CHUNK)
        xs = act_scr[rows, 0:D_SSM]
        dt = dt_scr[rows, :]
        adt = dt * a_row
        hi = adt.astype(BF16)
        r1 = adt - hi.astype(F32)
        mid = r1.astype(BF16)
        lo = (r1 - mid.astype(F32)).astype(BF16)
        acs = jnp.dot(tril_ref[...], jnp.concatenate([hi, mid, lo], axis=0),
                      preferred_element_type=F32)
        acs_last = acs[CHUNK - 1:CHUNK, :]
        stacked = jnp.concatenate([dt, jnp.exp(acs_last - acs), jnp.exp(acs)], axis=0)
        s_hi, s_lo = _split2(stacked)
        expanded = jnp.dot(jnp.concatenate([s_hi, s_lo], axis=1), exph_ref[...],
                           preferred_element_type=F32)
        dt_exp = expanded[0:CHUNK]
        dte_exp = expanded[CHUNK:2 * CHUNK]
        dfs_exp = expanded[2 * CHUNK:3 * CHUNK]
        xdt = xs * dt_exp
        xw_b = (xdt * dte_exp).astype(BF16)
        x_even = jnp.where(low_half, xdt, 0.0).astype(BF16)
        x_odd = jnp.where(low_half, 0.0, xdt).astype(BF16)
        acs_row = acs.T

        y_tiles = []
        for g in range(SSD_GROUPS):
            gcols = slice(g * GROUP_W, (g + 1) * GROUP_W)
            b_g = act_scr[rows, D_SSM + g * D_STATE:D_SSM + (g + 1) * D_STATE]
            c_g = act_scr[rows, D_SSM + SSD_GROUPS * D_STATE + g * D_STATE:
                          D_SSM + SSD_GROUPS * D_STATE + (g + 1) * D_STATE].astype(BF16)
            bt_g = b_g.T.astype(BF16)
            cb = jnp.dot(c_g, bt_g, preferred_element_type=F32)
            s_new = jnp.dot(bt_g, xw_b[:, gcols], preferred_element_type=F32)
            s_in = state_scr[g]
            y_off = jnp.dot(c_g, s_in.astype(BF16), preferred_element_type=F32) * dfs_exp[:, gcols]
            state_scr[g] = s_in * dfs_exp[CHUNK - 1:CHUNK, gcols] + s_new
            for pair in range(heads_per_group // 2):
                pcols = slice(g * GROUP_W + pair * LANES, g * GROUP_W + (pair + 1) * LANES)
                w2 = []
                for parity in range(2):
                    hh = g * heads_per_group + 2 * pair + parity
                    seg = jnp.where(causal, jnp.exp(acs[:, hh:hh + 1] - acs_row[hh:hh + 1, :]), 0.0)
                    w2.append((cb * seg).astype(BF16))
                y_diag = jnp.dot(jnp.concatenate(w2, axis=1),
                                 jnp.concatenate([x_even[:, pcols], x_odd[:, pcols]], axis=0),
                                 preferred_element_type=F32)
                y_tiles.append(y_diag + y_off[:, pair * LANES:(pair + 1) * LANES])
        y = (jnp.concatenate(y_tiles, axis=1) + dskip_ref[...] * xs) * _silu(z_scr[rows, :])
        return jnp.concatenate(
            [_rms(y[:, g * GROUP_W:(g + 1) * GROUP_W], norm_ref[:, g * GROUP_W:(g + 1) * GROUP_W])
             for g in range(SSD_GROUPS)], axis=1).astype(BF16)

    y_ssd = jnp.concatenate([chunk_body(c) for c in range(n_chunks)], axis=0)
    o_ref[0] = x + jnp.dot(jnp.concatenate([y_ssd] + y_pool, axis=1), wout_ref[...],
                           preferred_element_type=F32)


def _head_expand_matrix():
    m = np.zeros((2 * LANES, D_SSM), np.float32)
    for hh in range(SSD_HEADS):
        m[hh, hh * SSD_HEAD_DIM:(hh + 1) * SSD_HEAD_DIM] = 1.0
        m[LANES + hh, hh * SSD_HEAD_DIM:(hh + 1) * SSD_HEAD_DIM] = 1.0
    return jnp.asarray(m, BF16)


def _even_mixer(x3, gain, in_proj, conv_w, conv_b, dt_bias, a_log, d_skip, ssd_norm,
                pool_w, pool_scale, out_proj):
    b, t, d = x3.shape
    tm = TM_EVEN
    assert t % tm == 0 and tm % CHUNK == 0
    o_dt = D_SSM + CONV_CH
    o_pool = o_dt + SSD_HEADS
    win = jnp.concatenate(
        [in_proj[:, :o_dt], in_proj[:, o_pool:], in_proj[:, o_dt:o_pool],
         jnp.zeros((d, LANES - SSD_HEADS), in_proj.dtype)], axis=1).astype(BF16)
    n_in = win.shape[1]
    pad = LANES - SSD_HEADS
    dtb = jnp.pad(dt_bias, (0, pad)).reshape(1, LANES)
    alog = jnp.pad(a_log, (0, pad)).reshape(1, LANES)
    dskip = jnp.repeat(d_skip, SSD_HEAD_DIM).reshape(1, D_SSM)
    tril3 = jnp.asarray(np.tile(np.tril(np.ones((CHUNK, CHUNK), np.float32)), (1, 3)), BF16)
    n_g = len(POOL_WINDOWS)
    return pl.pallas_call(
        _even_kernel,
        grid=(b, t // tm),
        in_specs=[
            pl.BlockSpec((1, tm, d), lambda i, j: (i, j, 0)),
            _resident((1, d)),
            _resident((d, n_in)),
            _resident((CONV_WIDTH, CONV_CH)),
            _resident((1, CONV_CH)),
            _resident((1, LANES)),
            _resident((1, LANES)),
            _resident((1, D_SSM)),
            _resident((1, D_SSM)),
            _resident((n_g, POOL_GROUP, POOL_GROUP)),
            _resident((1, D_POOL)),
            _resident((D_SSM + D_POOL, d)),
            _resident((2 * LANES, D_SSM)),
            _resident((CHUNK, 3 * CHUNK)),
        ],
        out_specs=pl.BlockSpec((1, tm, d), lambda i, j: (i, j, 0)),
        out_shape=jax.ShapeDtypeStruct((b, t, d), F32),
        scratch_shapes=[
            pltpu.VMEM((CONV_TILES, tm + CONV_HALO, LANES), F32),
            pltpu.VMEM((tm, CONV_CH), F32),
            pltpu.VMEM((tm, LANES), F32),
            pltpu.VMEM((tm, D_SSM), F32),
            pltpu.VMEM((POOL_TILES, tm + POOL_HALO, LANES), F32),
            pltpu.VMEM((SSD_GROUPS, D_STATE, GROUP_W), F32),
            pltpu.VMEM((2, CONV_TILES, CONV_HALO, LANES), F32),
            pltpu.VMEM((2, POOL_TILES, POOL_HALO, LANES), F32),
        ],
        compiler_params=_params(2),
        name="even_mixer",
    )(x3, gain.reshape(1, d), win, conv_w, conv_b.reshape(1, CONV_CH), dtb, alog, dskip,
      ssd_norm.reshape(1, D_SSM), pool_w.reshape(n_g, POOL_GROUP, POOL_GROUP),
      pool_scale.reshape(1, D_POOL), out_proj, _head_expand_matrix(), tril3)


N_QK_HEADS = ATTN_HEADS + ATTN_KV_HEADS
QK_W = N_QK_HEADS * ATTN_HEAD_DIM
Q_W = ATTN_HEADS * ATTN_HEAD_DIM
KV_W = ATTN_KV_HEADS * ATTN_HEAD_DIM
PAIRS = ATTN_GROUP // 2
STACK = PAIRS * BLOCK
WO_BLOCKS = 4


def _odd_kernel(sink_ref, x_ref, g_ref, wqkv_ref, bqkv_ref, gqk_ref, headmean_ref, bias_ref,
                wo_ref, bo_ref, o_ref, q_scr, k_scr, v_scr, k_prev, v_prev):
    t = pl.program_id(1)
    tm = x_ref.shape[1]
    n_blocks = tm // BLOCK
    n_forms = 2 * ATTN_KV_HEADS

    slot = lax.rem(t, 2)

    @pl.when(t == 0)
    def _():
        k_prev[0] = jnp.zeros((n_forms, BLOCK, LANES), BF16)
        v_prev[0] = jnp.zeros((n_forms, BLOCK, 2 * LANES), BF16)

    k_scr[:, 0:BLOCK, :] = k_prev[slot]
    v_scr[:, 0:BLOCK, :] = v_prev[slot]

    x = x_ref[0]
    h = _rms(x, g_ref[...]).astype(BF16)
    qkv = jnp.dot(h, wqkv_ref[...], preferred_element_type=F32) + bqkv_ref[...]
    qk = qkv[:, 0:QK_W]
    sq = (qk * qk).astype(BF16)
    msq = jnp.concatenate(
        [jnp.dot(sq[:, c0:c0 + MXU_WIDTH], headmean_ref[...], preferred_element_type=F32)
         for c0 in range(0, QK_W, MXU_WIDTH)], axis=1)
    qkn = qk * lax.rsqrt(msq + EPS) * gqk_ref[...]
    q_scr[...] = qkn[:, 0:Q_W].astype(BF16)

    low_half = lax.broadcasted_iota(jnp.int32, (tm, LANES), 1) < HALF
    zeros = jnp.zeros((tm, LANES), F32)
    ones_lo = jnp.where(low_half, 1.0, 0.0).astype(BF16)
    ones_hi = jnp.where(low_half, 0.0, 1.0).astype(BF16)
    for is_v, src, dst, base in ((False, qkn, k_scr, Q_W), (True, qkv, v_scr, QK_W)):
        for pair in range(ATTN_KV_HEADS // 2):
            a2 = src[:, base + pair * LANES:base + (pair + 1) * LANES]
            a2r = pltpu.roll(a2, HALF, axis=1)
            forms = (
                jnp.where(low_half, a2, zeros), jnp.where(low_half, zeros, a2r),
                jnp.where(low_half, a2r, zeros), jnp.where(low_half, zeros, a2),
            )
            for f, val in enumerate(forms):
                dst[4 * pair + f, BLOCK:BLOCK + tm, 0:LANES] = val.astype(BF16)
                if is_v:
                    dst[4 * pair + f, BLOCK:BLOCK + tm, LANES:2 * LANES] = (
                        ones_lo if f % 2 == 0 else ones_hi)

    nt_dims = (((1,), (1,)), ((), ()))
    row = lax.broadcasted_iota(jnp.int32, (STACK, LANES), 0)
    col = lax.broadcasted_iota(jnp.int32, (STACK, LANES), 1)
    tri = col <= (row & (BLOCK - 1))
    lane_low = col < HALF
    row_pair0 = lax.broadcasted_iota(jnp.int32, (STACK, 1), 0) < BLOCK

    def block_attention(i):
        rows = pl.ds(i * BLOCK, BLOCK)
        krows = pl.ds(i * BLOCK, 2 * BLOCK)
        sel = jnp.where(t == 0, 1, 0) if i == 0 else 0
        att_tiles = []
        for kvh in range(ATTN_KV_HEADS):
            col0 = kvh * ATTN_GROUP * ATTN_HEAD_DIM
            q2 = jnp.concatenate(
                [q_scr[rows, col0 + p * LANES:col0 + (p + 1) * LANES] for p in range(PAIRS)],
                axis=0)
            acc = None
            shifts = []
            for parity in range(2):
                s2 = lax.dot_general(q2, k_scr[2 * kvh + parity, krows, :], nt_dims,
                                     preferred_element_type=F32)
                sc = jnp.where(tri, s2[:, BLOCK:], s2[:, :BLOCK]) + bias_ref[sel, kvh, parity]
                sink = jnp.where(row_pair0, sink_ref[kvh * ATTN_GROUP + parity],
                                 sink_ref[kvh * ATTN_GROUP + 2 + parity])
                m = jnp.maximum(jnp.max(sc, axis=-1, keepdims=True), sink)
                p = jnp.exp(sc - m)
                pcat = jnp.concatenate([jnp.where(tri, 0.0, p), jnp.where(tri, p, 0.0)],
                                       axis=1).astype(BF16)
                part = jnp.dot(pcat, v_scr[2 * kvh + parity, krows, :],
                               preferred_element_type=F32)
                acc = part if acc is None else acc + part
                shifts.append(sink - m)
            denom = acc[:, LANES:] + jnp.exp(jnp.where(lane_low, shifts[0], shifts[1]))
            att = (acc[:, :LANES] * (1.0 / denom)).astype(BF16)
            att_tiles.extend(att[p * BLOCK:(p + 1) * BLOCK] for p in range(PAIRS))
        return jnp.concatenate(att_tiles, axis=1)

    for i0 in range(0, n_blocks, WO_BLOCKS):
        att = jnp.concatenate([block_attention(i) for i in range(i0, i0 + WO_BLOCKS)], axis=0)
        rows = pl.ds(i0 * BLOCK, WO_BLOCKS * BLOCK)
        o_ref[0, rows, :] = (x_ref[0, rows, :] + bo_ref[...]
                             + jnp.dot(att, wo_ref[...], preferred_element_type=F32))
    k_prev[1 - slot] = k_scr[:, tm:tm + BLOCK, :]
    v_prev[1 - slot] = v_scr[:, tm:tm + BLOCK, :]


def _alibi_bias():
    slopes = 2.0 ** (-8.0 * (np.arange(ATTN_HEADS) + 1) / ATTN_HEADS)
    q = np.arange(BLOCK)[:, None]
    j = np.arange(BLOCK)[None, :]
    cur = j <= q
    dist = np.where(cur, q - j, q - j + BLOCK).astype(np.float32)
    assert dist.min() >= 0 and dist.max() < WINDOW
    out = np.zeros((2, ATTN_KV_HEADS, 2, STACK, BLOCK), np.float32)
    for kvh in range(ATTN_KV_HEADS):
        for parity in range(2):
            for p in range(PAIRS):
                head = kvh * ATTN_GROUP + 2 * p + parity
                base = -np.float32(slopes[head]) * dist
                out[0, kvh, parity, p * BLOCK:(p + 1) * BLOCK] = base
                out[1, kvh, parity, p * BLOCK:(p + 1) * BLOCK] = np.where(cur, base, -np.inf)
    return jnp.asarray(out)


def _head_mean_matrix():
    idx = np.arange(MXU_WIDTH) // ATTN_HEAD_DIM
    return jnp.asarray((idx[:, None] == idx[None, :]).astype(np.float32) / ATTN_HEAD_DIM, BF16)


def _odd_mixer(x3, gain, wqkv, bqkv, qnorm, knorm, sinks, wo, bo):
    b, t, d = x3.shape
    tm = TM_ODD
    assert t % tm == 0 and tm % (WO_BLOCKS * BLOCK) == 0 and QK_W % MXU_WIDTH == 0
    qkv_w = wqkv.shape[1]
    gqk = jnp.concatenate([jnp.tile(qnorm, ATTN_HEADS) * (ATTN_HEAD_DIM ** -0.5),
                           jnp.tile(knorm, ATTN_KV_HEADS)]).reshape(1, QK_W)
    n_forms = 2 * ATTN_KV_HEADS
    return pl.pallas_call(
        _odd_kernel,
        grid=(b, t // tm),
        in_specs=[
            pl.BlockSpec(memory_space=pltpu.SMEM),
            pl.BlockSpec((1, tm, d), lambda i, j: (i, j, 0)),
            _resident((1, d)),
            _resident((d, qkv_w)),
            _resident((1, qkv_w)),
            _resident((1, QK_W)),
            _resident((MXU_WIDTH, MXU_WIDTH)),
            _resident((2, ATTN_KV_HEADS, 2, STACK, BLOCK)),
            _resident((Q_W, d)),
            _resident((1, d)),
        ],
        out_specs=pl.BlockSpec((1, tm, d), lambda i, j: (i, j, 0)),
        out_shape=jax.ShapeDtypeStruct((b, t, d), F32),
        scratch_shapes=[
            pltpu.VMEM((tm, Q_W), BF16),
            pltpu.VMEM((n_forms, tm + BLOCK, LANES), BF16),
            pltpu.VMEM((n_forms, tm + BLOCK, 2 * LANES), BF16),
            pltpu.VMEM((2, n_forms, BLOCK, LANES), BF16),
            pltpu.VMEM((2, n_forms, BLOCK, 2 * LANES), BF16),
        ],
        compiler_params=_params(2),
        name="odd_mixer",
    )(sinks, x3, gain.reshape(1, d), wqkv, bqkv.reshape(1, qkv_w), gqk,
      _head_mean_matrix(), _alibi_bias(), wo, bo.reshape(1, d))


def kernel(x, mem, mem_norm, mem_wkv, mem_knorm, ffn1_norm, ffn1_wi, ffn1_wo, mix_norm, ssd_in_proj, ssd_conv_w, ssd_conv_b, ssd_dt_bias, ssd_a_log, ssd_d, ssd_norm, pool_w, pool_scale, even_out_proj, attn_wqkv, attn_bqkv, attn_qnorm, attn_knorm, attn_sinks, attn_wo, attn_bo, xattn_norm, xattn_wq, xattn_qnorm, xattn_wo, ffn2_norm, ffn2_wi, ffn2_wo):
    b, t, d = x.shape
    depth = ffn1_norm.shape[0]
    mem_kt, mem_v = _mem_kv(mem, mem_norm, mem_wkv, mem_knorm)
    pool_w2 = pool_w.reshape(pool_w.shape[0], D_POOL, POOL_GROUP)
    wi1, wo1 = _cast_weights([(ffn1_wi, 0), (ffn1_wo, 0)])
    for i in range(depth):
        jobs = [(ffn2_wi, i), (ffn2_wo, i), (xattn_wq, i), (xattn_wo, i)]
        if i % 2 == 0:
            jobs += [(pool_w2, i // 2), (even_out_proj, i // 2)]
        else:
            jobs += [(attn_wqkv, i // 2), (attn_wo, i // 2)]
        x2, (wi2, wo2, xq, xo, mix_a, mix_b) = _ffn(x.reshape(b * t, d), ffn1_norm[i], wi1, wo1, jobs)
        x = x2.reshape(b, t, d)
        if i % 2 == 0:
            e = i // 2
            x = _even_mixer(x, mix_norm[i], ssd_in_proj[e], ssd_conv_w[e], ssd_conv_b[e],
                            ssd_dt_bias[e], ssd_a_log[e], ssd_d[e], ssd_norm[e], mix_a,
                            pool_scale[e], mix_b)
        else:
            o = i // 2
            x = _odd_mixer(x, mix_norm[i], mix_a, attn_bqkv[o], attn_qnorm[o], attn_knorm[o],
                           attn_sinks[o], mix_b, attn_bo[o])
        x = _xattn(x, xattn_norm[i], xq, xattn_qnorm[i], mem_kt, mem_v, xo)
        jobs = [(ffn1_wi, i + 1), (ffn1_wo, i + 1)] if i + 1 < depth else []
        x2, nxt = _ffn(x.reshape(b * t, d), ffn2_norm[i], wi2, wo2, jobs)
        x = x2.reshape(b, t, d)
        if nxt:
            wi1, wo1 = nxt
    return x
```

```python
import numpy as np
import jax
import jax.numpy as jnp
from jax import lax
from jax.experimental import pallas as pl
from jax.experimental.pallas import tpu as pltpu

F32 = jnp.float32
BF16 = jnp.bfloat16

EPS = 1e-6
FFN_RESIDUAL = 0.5
SSD_HEADS = 16
SSD_HEAD_DIM = 64
SSD_GROUPS = 2
D_STATE = 128
CONV_WIDTH = 4
CHUNK = 128
D_SSM = SSD_HEADS * SSD_HEAD_DIM
GROUP_W = D_SSM // SSD_GROUPS
CONV_CH = D_SSM + 2 * SSD_GROUPS * D_STATE
POOL_WINDOWS = (2, 4, 8, 16)
POOL_GROUP = 256
D_POOL = POOL_GROUP * len(POOL_WINDOWS)
POOL_HALO = 16
CONV_HALO = 8
ATTN_HEADS = 16
ATTN_KV_HEADS = 4
ATTN_GROUP = ATTN_HEADS // ATTN_KV_HEADS
ATTN_HEAD_DIM = 64
WINDOW = 128
BLOCK = 128
MEM_HEADS = 4

LANES = 128
SUBLANES = 8
HALF = 64
MXU_WIDTH = 256
BF16_ROWS = 16
CAST_STEPS = 8

TM_FFN = 1024
TM_XATTN = 1024
TM_EVEN = 512
TM_ODD = 1024
VMEM_LIMIT_BYTES = 56 * 1024 * 1024
FFN_CHUNKS = (1024, 1024, 768)


def _rms(x, gain):
    ms = jnp.mean(x * x, axis=-1, keepdims=True)
    return x * lax.rsqrt(ms + EPS) * gain


def _silu(x):
    return x * (1.0 / (1.0 + jnp.exp(-x)))


def _split2(x):
    hi = x.astype(BF16)
    lo = (x - hi.astype(F32)).astype(BF16)
    return hi, lo


def _resident(shape):
    nd = len(shape)
    return pl.BlockSpec(shape, lambda *_: (0,) * nd, pipeline_mode=pl.Buffered(1))


def _params(n_axes):
    return pltpu.CompilerParams(
        dimension_semantics=("arbitrary",) * n_axes,
        vmem_limit_bytes=VMEM_LIMIT_BYTES,
    )


def _memkv_kernel(mem_ref, g_ref, wkv_ref, kn_ref, kt_ref, v_ref):
    d = mem_ref.shape[-1]
    hd = d // MEM_HEADS
    h = _rms(mem_ref[0], g_ref[...]).astype(BF16)
    kv = jnp.dot(h, wkv_ref[...], preferred_element_type=F32)
    for i in range(MEM_HEADS):
        kh = _rms(kv[:, i * hd:(i + 1) * hd], kn_ref[...])
        kt_ref[0, i] = kh.T.astype(BF16)
    v_ref[0] = kv[:, d:].astype(BF16)


def _mem_kv(mem, mem_norm, wkv, mem_knorm):
    b, m, d = mem.shape
    hd = d // MEM_HEADS
    return pl.pallas_call(
        _memkv_kernel,
        grid=(b,),
        in_specs=[
            pl.BlockSpec((1, m, d), lambda i: (i, 0, 0)),
            _resident((1, d)),
            _resident((d, 2 * d)),
            _resident((1, hd)),
        ],
        out_specs=[
            pl.BlockSpec((1, MEM_HEADS, hd, m), lambda i: (i, 0, 0, 0)),
            pl.BlockSpec((1, m, d), lambda i: (i, 0, 0)),
        ],
        out_shape=[
            jax.ShapeDtypeStruct((b, MEM_HEADS, hd, m), BF16),
            jax.ShapeDtypeStruct((b, m, d), BF16),
        ],
        compiler_params=_params(1),
        name="mem_kv",
    )(mem, mem_norm.reshape(1, d), wkv.astype(BF16), mem_knorm.reshape(1, hd))


def _cast_job_specs(jobs, n_steps):
    in_specs, out_specs, out_shapes = [], [], []
    for src, layer in jobs:
        rows, cols = src.shape[-2:]
        rb = next(r for r in range(BF16_ROWS, rows + 1, BF16_ROWS)
                  if rows % r == 0 and rows // r <= n_steps)
        last = rows // rb - 1
        if layer is None:
            in_specs.append(pl.BlockSpec((rb, cols), lambda i, last=last: (jnp.minimum(i, last), 0)))
        else:
            in_specs.append(pl.BlockSpec(
                (None, rb, cols), lambda i, last=last, layer=layer: (layer, jnp.minimum(i, last), 0)))
        out_specs.append(pl.BlockSpec((rb, cols), lambda i, last=last: (jnp.minimum(i, last), 0)))
        out_shapes.append(jax.ShapeDtypeStruct((rows, cols), BF16))
    return in_specs, out_specs, out_shapes


def _run_cast_jobs(in_refs, out_refs):
    for src_ref, dst_ref in zip(in_refs, out_refs):
        dst_ref[...] = src_ref[...].astype(BF16)


def _cast_kernel(*refs):
    n = len(refs) // 2
    _run_cast_jobs(refs[:n], refs[n:])


def _cast_weights(jobs):
    in_specs, out_specs, out_shapes = _cast_job_specs(jobs, CAST_STEPS)
    return pl.pallas_call(
        _cast_kernel,
        grid=(CAST_STEPS,),
        in_specs=in_specs,
        out_specs=out_specs,
        out_shape=out_shapes,
        compiler_params=_params(1),
        name="cast_weights",
    )(*[src for src, _ in jobs])


def _ffn_kernel(x_ref, g_ref, wi_ref, wo_ref, *refs):
    n_jobs = (len(refs) - 2) // 2
    o_ref, act_ref = refs[n_jobs], refs[-1]
    _run_cast_jobs(refs[:n_jobs], refs[n_jobs + 1:2 * n_jobs + 1])
    d_ff = wo_ref.shape[0]
    h = _rms(x_ref[...], g_ref[...]).astype(BF16)
    lo = 0
    for ck in FFN_CHUNKS:
        gate = jnp.dot(h, wi_ref[:, lo:lo + ck], preferred_element_type=F32)
        up = jnp.dot(h, wi_ref[:, d_ff + lo:d_ff + lo + ck], preferred_element_type=F32)
        act_ref[:, lo:lo + ck] = (_silu(gate) * up).astype(BF16)
        lo += ck
    y = jnp.dot(act_ref[...], wo_ref[...], preferred_element_type=F32)
    o_ref[...] = x_ref[...] + FFN_RESIDUAL * y


def _ffn(x2, gain, wi, wo, cast_jobs=()):
    n, d = x2.shape
    d_ff = wo.shape[0]
    assert sum(FFN_CHUNKS) == d_ff and n % TM_FFN == 0
    n_steps = n // TM_FFN
    job_in, job_out, job_shapes = _cast_job_specs(cast_jobs, n_steps)
    outs = pl.pallas_call(
        _ffn_kernel,
        grid=(n_steps,),
        in_specs=[
            pl.BlockSpec((TM_FFN, d), lambda i: (i, 0)),
            _resident((1, d)),
            _resident((d, 2 * d_ff)),
            _resident((d_ff, d)),
        ] + job_in,
        out_specs=[pl.BlockSpec((TM_FFN, d), lambda i: (i, 0))] + job_out,
        out_shape=[jax.ShapeDtypeStruct((n, d), F32)] + job_shapes,
        scratch_shapes=[pltpu.VMEM((TM_FFN, d_ff), BF16)],
        compiler_params=_params(1),
        name="ffn",
    )(x2, gain.reshape(1, d), wi, wo, *[src for src, _ in cast_jobs])
    return outs[0], outs[1:]


def _xattn_kernel(x_ref, g_ref, wq_ref, qn_ref, kt_ref, v_ref, wo_ref, o_ref, att_ref):
    d = x_ref.shape[-1]
    hd = d // MEM_HEADS
    h = _rms(x_ref[0], g_ref[...]).astype(BF16)
    q = jnp.dot(h, wq_ref[...], preferred_element_type=F32)
    for i in range(MEM_HEADS):
        cols = slice(i * hd, (i + 1) * hd)
        qh = (_rms(q[:, cols], qn_ref[...]) * (hd ** -0.5)).astype(BF16)
        s = jnp.dot(qh, kt_ref[0, i], preferred_element_type=F32)
        p = jnp.exp(s - jnp.max(s, axis=-1, keepdims=True))
        inv = 1.0 / jnp.sum(p, axis=-1, keepdims=True)
        o = jnp.dot(p.astype(BF16), v_ref[0, :, cols], preferred_element_type=F32)
        att_ref[:, cols] = (o * inv).astype(BF16)
    y = jnp.dot(att_ref[...], wo_ref[...], preferred_element_type=F32)
    o_ref[0] = x_ref[0] + y


def _xattn(x3, gain, wq, qnorm, mem_kt, mem_v, wo):
    b, t, d = x3.shape
    hd = d // MEM_HEADS
    m = mem_v.shape[1]
    tm = TM_XATTN
    assert t % tm == 0
    return pl.pallas_call(
        _xattn_kernel,
        grid=(b, t // tm),
        in_specs=[
            pl.BlockSpec((1, tm, d), lambda i, j: (i, j, 0)),
            _resident((1, d)),
            _resident((d, d)),
            _resident((1, hd)),
            pl.BlockSpec((1, MEM_HEADS, hd, m), lambda i, j: (i, 0, 0, 0)),
            pl.BlockSpec((1, m, d), lambda i, j: (i, 0, 0)),
            _resident((d, d)),
        ],
        out_specs=pl.BlockSpec((1, tm, d), lambda i, j: (i, j, 0)),
        out_shape=jax.ShapeDtypeStruct((b, t, d), F32),
        scratch_shapes=[pltpu.VMEM((tm, d), BF16)],
        compiler_params=_params(2),
        name="xattn",
    )(x3, gain.reshape(1, d), wq, qnorm.reshape(1, hd), mem_kt, mem_v, wo)


CONV_TILES = CONV_CH // LANES
POOL_TILES = D_POOL // LANES
TILES_PER_POOL_GROUP = POOL_GROUP // LANES


def _even_kernel(x_ref, g_ref, wzx_ref, wpool_ref, wdt_ref, cw_ref, cb_ref, dtb_ref, alog_ref,
                 dskip_ref, norm_ref,
                 pw_ref, ps_ref, wout_ref, exph_ref, tril_ref, o_ref,
                 conv_scr, act_scr, dt_scr, z_scr, pool_scr, state_scr, conv_halo, pool_halo):
    t = pl.program_id(1)
    tm = x_ref.shape[1]
    n_chunks = tm // CHUNK

    slot = lax.rem(t, 2)

    @pl.when(t == 0)
    def _():
        conv_halo[0] = jnp.zeros((CONV_TILES, CONV_HALO, LANES), F32)
        pool_halo[0] = jnp.zeros((POOL_TILES, POOL_HALO, LANES), F32)
        state_scr[...] = jnp.zeros(state_scr.shape, F32)

    conv_scr[:, 0:CONV_HALO, :] = conv_halo[slot]
    pool_scr[:, 0:POOL_HALO, :] = pool_halo[slot]

    x = x_ref[0]
    h = _rms(x, g_ref[...]).astype(BF16)
    xbc = jnp.dot(h, wzx_ref[:, D_SSM:], preferred_element_type=F32)
    for c in range(CONV_TILES):
        conv_scr[c, CONV_HALO:CONV_HALO + tm, :] = xbc[:, c * LANES:(c + 1) * LANES]
    u_pool = jnp.dot(h, wpool_ref[...], preferred_element_type=F32)
    for c in range(POOL_TILES):
        pool_scr[c, POOL_HALO:POOL_HALO + tm, :] = u_pool[:, c * LANES:(c + 1) * LANES]
    dt_raw = jnp.dot(h, wdt_ref[...], preferred_element_type=F32) + dtb_ref[...]
    dt_scr[...] = jnp.maximum(dt_raw, 0.0) + jnp.log1p(jnp.exp(-jnp.abs(dt_raw)))
    z_scr[...] = jnp.dot(h, wzx_ref[:, 0:D_SSM], preferred_element_type=F32)

    for c in range(CONV_TILES):
        cols = slice(c * LANES, (c + 1) * LANES)
        acc = cb_ref[:, cols]
        for k in range(CONV_WIDTH):
            acc = acc + (conv_scr[c, pl.ds(CONV_HALO - CONV_WIDTH + 1 + k, tm), :]
                         * cw_ref[k:k + 1, cols])
        act_scr[:, cols] = _silu(acc)
    conv_halo[1 - slot] = conv_scr[:, tm:tm + CONV_HALO, :]

    pos = t * tm + lax.broadcasted_iota(jnp.int32, (tm, 1), 0)
    y_pool = []
    for k, w in enumerate(POOL_WINDOWS):
        inv_count = 1.0 / jnp.minimum(pos + 1, w).astype(F32)
        pooled = []
        for c in range(k * TILES_PER_POOL_GROUP, (k + 1) * TILES_PER_POOL_GROUP):
            if w > SUBLANES:
                e = pool_scr[c, pl.ds(POOL_HALO - SUBLANES, tm + SUBLANES), :]
                for i in range(1, SUBLANES):
                    e = e + pool_scr[c, pl.ds(POOL_HALO - SUBLANES - i, tm + SUBLANES), :]
                s = e[SUBLANES:, :] + e[:tm, :]
            else:
                s = pool_scr[c, pl.ds(POOL_HALO, tm), :]
                for i in range(1, w):
                    s = s + pool_scr[c, pl.ds(POOL_HALO - i, tm), :]
            pooled.append(s * inv_count - pool_scr[c, pl.ds(POOL_HALO, tm), :])
        pooled = jnp.concatenate(pooled, axis=1).astype(BF16)
        kcols = slice(k * POOL_GROUP, (k + 1) * POOL_GROUP)
        yk = jnp.dot(pooled, pw_ref[k], preferred_element_type=F32)
        y_pool.append((yk * ps_ref[:, kcols]).astype(BF16))
    pool_halo[1 - slot] = pool_scr[:, tm:tm + POOL_HALO, :]

    a_row = -jnp.exp(alog_ref[...])
    low_half = (lax.broadcasted_iota(jnp.int32, (CHUNK, D_SSM), 1) & HALF) == 0
    causal = (lax.broadcasted_iota(jnp.int32, (CHUNK, CHUNK), 0)
              >= lax.broadcasted_iota(jnp.int32, (CHUNK, CHUNK), 1))
    heads_per_group = SSD_HEADS // SSD_GROUPS

    def chunk_body(c):
        rows = pl.ds(c * CHUNK, CHUNK)
        xs = act_scr[rows, 0:D_SSM]
        dt = dt_scr[rows, :]
        adt = dt * a_row
        hi = adt.astype(BF16)
        r1 = adt - hi.astype(F32)
        mid = r1.astype(BF16)
        lo = (r1 - mid.astype(F32)).astype(BF16)
        acs = jnp.dot(tril_ref[...], jnp.concatenate([hi, mid, lo], axis=0),
                      preferred_element_type=F32)
        acs_last = acs[CHUNK - 1:CHUNK, :]
        stacked = jnp.concatenate([dt, jnp.exp(acs_last - acs), jnp.exp(acs)], axis=0)
        s_hi, s_lo = _split2(stacked)
        expanded = jnp.dot(jnp.concatenate([s_hi, s_lo], axis=1), exph_ref[...],
                           preferred_element_type=F32)
        dt_exp = expanded[0:CHUNK]
        dte_exp = expanded[CHUNK:2 * CHUNK]
        dfs_exp = expanded[2 * CHUNK:3 * CHUNK]
        xdt = xs * dt_exp
        xw_b = (xdt * dte_exp).astype(BF16)
        x_even = jnp.where(low_half, xdt, 0.0).astype(BF16)
        x_odd = jnp.where(low_half, 0.0, xdt).astype(BF16)
        acs_row = acs.T

        y_tiles = []
        for g in range(SSD_GROUPS):
            gcols = slice(g * GROUP_W, (g + 1) * GROUP_W)
            b_g = act_scr[rows, D_SSM + g * D_STATE:D_SSM + (g + 1) * D_STATE]
            c_g = act_scr[rows, D_SSM + SSD_GROUPS * D_STATE + g * D_STATE:
                          D_SSM + SSD_GROUPS * D_STATE + (g + 1) * D_STATE].astype(BF16)
            bt_g = b_g.T.astype(BF16)
            cb = jnp.dot(c_g, bt_g, preferred_element_type=F32)
            s_new = jnp.dot(bt_g, xw_b[:, gcols], preferred_element_type=F32)
            s_in = state_scr[g]
            y_off = jnp.dot(c_g, s_in.astype(BF16), preferred_element_type=F32) * dfs_exp[:, gcols]
            state_scr[g] = s_in * dfs_exp[CHUNK - 1:CHUNK, gcols] + s_new
            for pair in range(heads_per_group // 2):
                pcols = slice(g * GROUP_W + pair * LANES, g * GROUP_W + (pair + 1) * LANES)
                w2 = []
                for parity in range(2):
                    hh = g * heads_per_group + 2 * pair + parity
                    seg = jnp.where(causal, jnp.exp(acs[:, hh:hh + 1] - acs_row[hh:hh + 1, :]), 0.0)
                    w2.append((cb * seg).astype(BF16))
                y_diag = jnp.dot(jnp.concatenate(w2, axis=1),
                                 jnp.concatenate([x_even[:, pcols], x_odd[:, pcols]], axis=0),
                                 preferred_element_type=F32)
                y_tiles.append(y_diag + y_off[:, pair * LANES:(pair + 1) * LANES])
        y = (jnp.concatenate(y_tiles, axis=1) + dskip_ref[...] * xs) * _silu(z_scr[rows, :])
        return jnp.concatenate(
            [_rms(y[:, g * GROUP_W:(g + 1) * GROUP_W], norm_ref[:, g * GROUP_W:(g + 1) * GROUP_W])
             for g in range(SSD_GROUPS)], axis=1).astype(BF16)

    y_ssd = jnp.concatenate([chunk_body(c) for c in range(n_chunks)], axis=0)
    o_ref[0] = x + jnp.dot(jnp.concatenate([y_ssd] + y_pool, axis=1), wout_ref[...],
                           preferred_element_type=F32)


def _head_expand_matrix():
    m = np.zeros((2 * LANES, D_SSM), np.float32)
    for hh in range(SSD_HEADS):
        m[hh, hh * SSD_HEAD_DIM:(hh + 1) * SSD_HEAD_DIM] = 1.0
        m[LANES + hh, hh * SSD_HEAD_DIM:(hh + 1) * SSD_HEAD_DIM] = 1.0
    return jnp.asarray(m, BF16)


def _even_mixer(x3, gain, in_proj, conv_w, conv_b, dt_bias, a_log, d_skip, ssd_norm,
                pool_w, pool_scale, out_proj):
    b, t, d = x3.shape
    tm = TM_EVEN
    assert t % tm == 0 and tm % CHUNK == 0
    o_dt = D_SSM + CONV_CH
    o_pool = o_dt + SSD_HEADS
    pad = LANES - SSD_HEADS
    w_bf = in_proj.astype(BF16)
    w_zx, w_pool = w_bf[:, :o_dt], w_bf[:, o_pool:]
    w_dt = jnp.pad(w_bf[:, o_dt:o_pool], ((0, 0), (0, pad)))
    dtb = jnp.pad(dt_bias, (0, pad)).reshape(1, LANES)
    alog = jnp.pad(a_log, (0, pad)).reshape(1, LANES)
    dskip = jnp.repeat(d_skip, SSD_HEAD_DIM).reshape(1, D_SSM)
    tril3 = jnp.asarray(np.tile(np.tril(np.ones((CHUNK, CHUNK), np.float32)), (1, 3)), BF16)
    n_g = len(POOL_WINDOWS)
    return pl.pallas_call(
        _even_kernel,
        grid=(b, t // tm),
        in_specs=[
            pl.BlockSpec((1, tm, d), lambda i, j: (i, j, 0)),
            _resident((1, d)),
            _resident((d, o_dt)),
            _resident((d, D_POOL)),
            _resident((d, LANES)),
            _resident((CONV_WIDTH, CONV_CH)),
            _resident((1, CONV_CH)),
            _resident((1, LANES)),
            _resident((1, LANES)),
            _resident((1, D_SSM)),
            _resident((1, D_SSM)),
            _resident((n_g, POOL_GROUP, POOL_GROUP)),
            _resident((1, D_POOL)),
            _resident((D_SSM + D_POOL, d)),
            _resident((2 * LANES, D_SSM)),
            _resident((CHUNK, 3 * CHUNK)),
        ],
        out_specs=pl.BlockSpec((1, tm, d), lambda i, j: (i, j, 0)),
        out_shape=jax.ShapeDtypeStruct((b, t, d), F32),
        scratch_shapes=[
            pltpu.VMEM((CONV_TILES, tm + CONV_HALO, LANES), F32),
            pltpu.VMEM((tm, CONV_CH), F32),
            pltpu.VMEM((tm, LANES), F32),
            pltpu.VMEM((tm, D_SSM), F32),
            pltpu.VMEM((POOL_TILES, tm + POOL_HALO, LANES), F32),
            pltpu.VMEM((SSD_GROUPS, D_STATE, GROUP_W), F32),
            pltpu.VMEM((2, CONV_TILES, CONV_HALO, LANES), F32),
            pltpu.VMEM((2, POOL_TILES, POOL_HALO, LANES), F32),
        ],
        compiler_params=_params(2),
        name="even_mixer",
    )(x3, gain.reshape(1, d), w_zx, w_pool, w_dt, conv_w, conv_b.reshape(1, CONV_CH), dtb, alog,
      dskip,
      ssd_norm.reshape(1, D_SSM), pool_w.reshape(n_g, POOL_GROUP, POOL_GROUP),
      pool_scale.reshape(1, D_POOL), out_proj, _head_expand_matrix(), tril3)


N_QK_HEADS = ATTN_HEADS + ATTN_KV_HEADS
QK_W = N_QK_HEADS * ATTN_HEAD_DIM
Q_W = ATTN_HEADS * ATTN_HEAD_DIM
KV_W = ATTN_KV_HEADS * ATTN_HEAD_DIM
KV_PAIRS = ATTN_KV_HEADS // 2
STACK = ATTN_GROUP * BLOCK
WO_BLOCKS = 4


def _odd_kernel(sink_ref, x_ref, g_ref, wqkv_ref, bqkv_ref, gqk_ref, headmean_ref, bias_ref,
                wo_ref, bo_ref, o_ref, q_scr, k_scr, v_scr, k_prev, v_prev):
    t = pl.program_id(1)
    tm = x_ref.shape[1]
    n_blocks = tm // BLOCK
    n_forms = 2 * KV_PAIRS

    slot = lax.rem(t, 2)

    @pl.when(t == 0)
    def _():
        k_prev[0] = jnp.zeros((n_forms, BLOCK, LANES), BF16)
        v_prev[0] = jnp.zeros((n_forms, BLOCK, 2 * LANES), BF16)

    k_scr[:, 0:BLOCK, :] = k_prev[slot]
    v_scr[:, 0:BLOCK, :] = v_prev[slot]

    x = x_ref[0]
    h = _rms(x, g_ref[...]).astype(BF16)
    qkv = jnp.dot(h, wqkv_ref[...], preferred_element_type=F32) + bqkv_ref[...]
    qk = qkv[:, 0:QK_W]
    sq = (qk * qk).astype(BF16)
    msq = jnp.concatenate(
        [jnp.dot(sq[:, c0:c0 + MXU_WIDTH], headmean_ref[...], preferred_element_type=F32)
         for c0 in range(0, QK_W, MXU_WIDTH)], axis=1)
    qkn = qk * lax.rsqrt(msq + EPS) * gqk_ref[...]
    q_scr[...] = qkn[:, 0:Q_W].astype(BF16)

    low_half = lax.broadcasted_iota(jnp.int32, (tm, LANES), 1) < HALF
    zeros = jnp.zeros((tm, LANES), F32)
    ones = (jnp.where(low_half, 1.0, 0.0).astype(BF16), jnp.where(low_half, 0.0, 1.0).astype(BF16))
    for is_v, src, dst, base in ((False, qkn, k_scr, Q_W), (True, qkv, v_scr, QK_W)):
        for kp in range(KV_PAIRS):
            a2 = src[:, base + kp * LANES:base + (kp + 1) * LANES]
            for f, val in enumerate((jnp.where(low_half, a2, zeros), jnp.where(low_half, zeros, a2))):
                dst[2 * kp + f, BLOCK:BLOCK + tm, 0:LANES] = val.astype(BF16)
                if is_v:
                    dst[2 * kp + f, BLOCK:BLOCK + tm, LANES:2 * LANES] = ones[f]

    nt_dims = (((1,), (1,)), ((), ()))
    row = lax.broadcasted_iota(jnp.int32, (STACK, LANES), 0)
    col = lax.broadcasted_iota(jnp.int32, (STACK, LANES), 1)
    tri = col <= (row & (BLOCK - 1))
    lane_low = col < HALF
    row_group = lax.broadcasted_iota(jnp.int32, (STACK, 1), 0) // BLOCK

    def sink_column(kvh):
        sink = jnp.full((STACK, 1), sink_ref[kvh * ATTN_GROUP], F32)
        for g in range(1, ATTN_GROUP):
            sink = jnp.where(row_group == g, sink_ref[kvh * ATTN_GROUP + g], sink)
        return sink

    def block_attention(i):
        rows = pl.ds(i * BLOCK, BLOCK)
        krows = pl.ds(i * BLOCK, 2 * BLOCK)
        sel = jnp.where(t == 0, 1, 0) if i == 0 else 0
        att_tiles = []
        for kp in range(KV_PAIRS):
            col0 = kp * ATTN_GROUP * LANES
            q4 = jnp.concatenate(
                [q_scr[rows, col0 + g * LANES:col0 + (g + 1) * LANES] for g in range(ATTN_GROUP)],
                axis=0)
            acc = None
            shifts = []
            for f in range(2):
                s2 = lax.dot_general(q4, k_scr[2 * kp + f, krows, :], nt_dims,
                                     preferred_element_type=F32)
                sc = jnp.where(tri, s2[:, BLOCK:], s2[:, :BLOCK]) + bias_ref[sel, kp, f]
                sink = sink_column(2 * kp + f)
                m = jnp.maximum(jnp.max(sc, axis=-1, keepdims=True), sink)
                p = jnp.exp(sc - m)
                pcat = jnp.concatenate([jnp.where(tri, 0.0, p), jnp.where(tri, p, 0.0)],
                                       axis=1).astype(BF16)
                part = jnp.dot(pcat, v_scr[2 * kp + f, krows, :],
                               preferred_element_type=F32)
                acc = part if acc is None else acc + part
                shifts.append(sink - m)
            denom = acc[:, LANES:] + jnp.exp(jnp.where(lane_low, shifts[0], shifts[1]))
            att = (acc[:, :LANES] * (1.0 / denom)).astype(BF16)
            att_tiles.extend(att[g * BLOCK:(g + 1) * BLOCK] for g in range(ATTN_GROUP))
        return jnp.concatenate(att_tiles, axis=1)

    for i0 in range(0, n_blocks, WO_BLOCKS):
        att = jnp.concatenate([block_attention(i) for i in range(i0, i0 + WO_BLOCKS)], axis=0)
        rows = pl.ds(i0 * BLOCK, WO_BLOCKS * BLOCK)
        o_ref[0, rows, :] = (x_ref[0, rows, :] + bo_ref[...]
                             + jnp.dot(att, wo_ref[...], preferred_element_type=F32))
    k_prev[1 - slot] = k_scr[:, tm:tm + BLOCK, :]
    v_prev[1 - slot] = v_scr[:, tm:tm + BLOCK, :]


def _alibi_bias():
    slopes = 2.0 ** (-8.0 * (np.arange(ATTN_HEADS) + 1) / ATTN_HEADS)
    q = np.arange(BLOCK)[:, None]
    j = np.arange(BLOCK)[None, :]
    cur = j <= q
    dist = np.where(cur, q - j, q - j + BLOCK).astype(np.float32)
    assert dist.min() >= 0 and dist.max() < WINDOW
    out = np.zeros((2, KV_PAIRS, 2, STACK, BLOCK), np.float32)
    for kp in range(KV_PAIRS):
        for f in range(2):
            for g in range(ATTN_GROUP):
                head = (2 * kp + f) * ATTN_GROUP + g
                base = -np.float32(slopes[head]) * dist
                out[0, kp, f, g * BLOCK:(g + 1) * BLOCK] = base
                out[1, kp, f, g * BLOCK:(g + 1) * BLOCK] = np.where(cur, base, -np.inf)
    return jnp.asarray(out)


def _to_kernel_head_order(a, axis):
    shape = a.shape
    split = shape[:axis] + (KV_PAIRS, 2, ATTN_GROUP, ATTN_HEAD_DIM) + shape[axis + 1:]
    return jnp.swapaxes(a.reshape(split), axis + 1, axis + 2).reshape(shape)


def _head_mean_matrix():
    idx = np.arange(MXU_WIDTH) // ATTN_HEAD_DIM
    return jnp.asarray((idx[:, None] == idx[None, :]).astype(np.float32) / ATTN_HEAD_DIM, BF16)


def _odd_mixer(x3, gain, wqkv, bqkv, qnorm, knorm, sinks, wo, bo):
    b, t, d = x3.shape
    tm = TM_ODD
    assert t % tm == 0 and tm % (WO_BLOCKS * BLOCK) == 0 and QK_W % MXU_WIDTH == 0
    qkv_w = wqkv.shape[1]
    gqk = jnp.concatenate([jnp.tile(qnorm, ATTN_HEADS) * (ATTN_HEAD_DIM ** -0.5),
                           jnp.tile(knorm, ATTN_KV_HEADS)]).reshape(1, QK_W)
    wqkv = jnp.concatenate([_to_kernel_head_order(wqkv[:, :Q_W], 1), wqkv[:, Q_W:]], axis=1)
    bqkv = jnp.concatenate([_to_kernel_head_order(bqkv[:Q_W], 0), bqkv[Q_W:]])
    wo = _to_kernel_head_order(wo, 0)
    n_forms = 2 * KV_PAIRS
    return pl.pallas_call(
        _odd_kernel,
        grid=(b, t // tm),
        in_specs=[
            pl.BlockSpec(memory_space=pltpu.SMEM),
            pl.BlockSpec((1, tm, d), lambda i, j: (i, j, 0)),
            _resident((1, d)),
            _resident((d, qkv_w)),
            _resident((1, qkv_w)),
            _resident((1, QK_W)),
            _resident((MXU_WIDTH, MXU_WIDTH)),
            _resident((2, KV_PAIRS, 2, STACK, BLOCK)),
            _resident((Q_W, d)),
            _resident((1, d)),
        ],
        out_specs=pl.BlockSpec((1, tm, d), lambda i, j: (i, j, 0)),
        out_shape=jax.ShapeDtypeStruct((b, t, d), F32),
        scratch_shapes=[
            pltpu.VMEM((tm, Q_W), BF16),
            pltpu.VMEM((n_forms, tm + BLOCK, LANES), BF16),
            pltpu.VMEM((n_forms, tm + BLOCK, 2 * LANES), BF16),
            pltpu.VMEM((2, n_forms, BLOCK, LANES), BF16),
            pltpu.VMEM((2, n_forms, BLOCK, 2 * LANES), BF16),
        ],
        compiler_params=_params(2),
        name="odd_mixer",
    )(sinks, x3, gain.reshape(1, d), wqkv, bqkv.reshape(1, qkv_w), gqk,
      _head_mean_matrix(), _alibi_bias(), wo, bo.reshape(1, d))


def kernel(x, mem, mem_norm, mem_wkv, mem_knorm, ffn1_norm, ffn1_wi, ffn1_wo, mix_norm, ssd_in_proj, ssd_conv_w, ssd_conv_b, ssd_dt_bias, ssd_a_log, ssd_d, ssd_norm, pool_w, pool_scale, even_out_proj, attn_wqkv, attn_bqkv, attn_qnorm, attn_knorm, attn_sinks, attn_wo, attn_bo, xattn_norm, xattn_wq, xattn_qnorm, xattn_wo, ffn2_norm, ffn2_wi, ffn2_wo):
    b, t, d = x.shape
    depth = ffn1_norm.shape[0]
    mem_kt, mem_v = _mem_kv(mem, mem_norm, mem_wkv, mem_knorm)
    pool_w2 = pool_w.reshape(pool_w.shape[0], D_POOL, POOL_GROUP)
    wi1, wo1 = _cast_weights([(ffn1_wi, 0), (ffn1_wo, 0)])
    for i in range(depth):
        jobs = [(ffn2_wi, i), (ffn2_wo, i), (xattn_wq, i), (xattn_wo, i)]
        if i % 2 == 0:
            jobs += [(pool_w2, i // 2), (even_out_proj, i // 2)]
        else:
            jobs += [(attn_wqkv, i // 2), (attn_wo, i // 2)]
        x2, (wi2, wo2, xq, xo, mix_a, mix_b) = _ffn(x.reshape(b * t, d), ffn1_norm[i], wi1, wo1, jobs)
        x = x2.reshape(b, t, d)
        if i % 2 == 0:
            e = i // 2
            x = _even_mixer(x, mix_norm[i], ssd_in_proj[e], ssd_conv_w[e], ssd_conv_b[e],
                            ssd_dt_bias[e], ssd_a_log[e], ssd_d[e], ssd_norm[e], mix_a,
                            pool_scale[e], mix_b)
        else:
            o = i // 2
            x = _odd_mixer(x, mix_norm[i], mix_a, attn_bqkv[o], attn_qnorm[o], attn_knorm[o],
                           attn_sinks[o], mix_b, attn_bo[o])
        x = _xattn(x, xattn_norm[i], xq, xattn_qnorm[i], mem_kt, mem_v, xo)
        jobs = [(ffn1_wi, i + 1), (ffn1_wo, i + 1)] if i + 1 < depth else []
        x2, nxt = _ffn(x.reshape(b * t, d), ffn2_norm[i], wi2, wo2, jobs)
        x = x2.reshape(b, t, d)
        if nxt:
            wi1, wo1 = nxt
    return x
```

```python
import numpy as np
import jax
import jax.numpy as jnp
from jax import lax
from jax.experimental import pallas as pl
from jax.experimental.pallas import tpu as pltpu

F32 = jnp.float32
BF16 = jnp.bfloat16

EPS = 1e-6
FFN_RESIDUAL = 0.5
SSD_HEADS = 16
SSD_HEAD_DIM = 64
SSD_GROUPS = 2
D_STATE = 128
CONV_WIDTH = 4
CHUNK = 128
D_SSM = SSD_HEADS * SSD_HEAD_DIM
GROUP_W = D_SSM // SSD_GROUPS
CONV_CH = D_SSM + 2 * SSD_GROUPS * D_STATE
POOL_WINDOWS = (2, 4, 8, 16)
POOL_GROUP = 256
D_POOL = POOL_GROUP * len(POOL_WINDOWS)
POOL_HALO = 16
CONV_HALO = 8
ATTN_HEADS = 16
ATTN_KV_HEADS = 4
ATTN_GROUP = ATTN_HEADS // ATTN_KV_HEADS
ATTN_HEAD_DIM = 64
WINDOW = 128
BLOCK = 128
MEM_HEADS = 4

LANES = 128
SUBLANES = 8
HALF = 64
MXU_WIDTH = 256
BF16_ROWS = 16
CAST_STEPS = 8

TM_FFN = 1024
TM_XATTN = 1024
TM_EVEN = 1024
TM_ODD = 1024
VMEM_LIMIT_BYTES = 56 * 1024 * 1024
FFN_CHUNKS = (1024, 1024, 768)


def _rms(x, gain):
    ms = jnp.mean(x * x, axis=-1, keepdims=True)
    return x * lax.rsqrt(ms + EPS) * gain


def _silu(x):
    return x * (1.0 / (1.0 + jnp.exp(-x)))


def _split2(x):
    hi = x.astype(BF16)
    lo = (x - hi.astype(F32)).astype(BF16)
    return hi, lo


def _resident(shape):
    nd = len(shape)
    return pl.BlockSpec(shape, lambda *_: (0,) * nd, pipeline_mode=pl.Buffered(1))


def _params(n_axes):
    return pltpu.CompilerParams(
        dimension_semantics=("arbitrary",) * n_axes,
        vmem_limit_bytes=VMEM_LIMIT_BYTES,
    )


def _memkv_kernel(mem_ref, g_ref, wkv_ref, kn_ref, kt_ref, v_ref):
    d = mem_ref.shape[-1]
    hd = d // MEM_HEADS
    h = _rms(mem_ref[0], g_ref[...]).astype(BF16)
    kv = jnp.dot(h, wkv_ref[...], preferred_element_type=F32)
    for i in range(MEM_HEADS):
        kh = _rms(kv[:, i * hd:(i + 1) * hd], kn_ref[...])
        kt_ref[0, i] = kh.T.astype(BF16)
    v_ref[0] = kv[:, d:].astype(BF16)


def _mem_kv(mem, mem_norm, wkv, mem_knorm):
    b, m, d = mem.shape
    hd = d // MEM_HEADS
    return pl.pallas_call(
        _memkv_kernel,
        grid=(b,),
        in_specs=[
            pl.BlockSpec((1, m, d), lambda i: (i, 0, 0)),
            _resident((1, d)),
            _resident((d, 2 * d)),
            _resident((1, hd)),
        ],
        out_specs=[
            pl.BlockSpec((1, MEM_HEADS, hd, m), lambda i: (i, 0, 0, 0)),
            pl.BlockSpec((1, m, d), lambda i: (i, 0, 0)),
        ],
        out_shape=[
            jax.ShapeDtypeStruct((b, MEM_HEADS, hd, m), BF16),
            jax.ShapeDtypeStruct((b, m, d), BF16),
        ],
        compiler_params=_params(1),
        name="mem_kv",
    )(mem, mem_norm.reshape(1, d), wkv.astype(BF16), mem_knorm.reshape(1, hd))


def _cast_job_specs(jobs, n_steps):
    in_specs, out_specs, out_shapes = [], [], []
    for src, layer in jobs:
        rows, cols = src.shape[-2:]
        rb = next(r for r in range(BF16_ROWS, rows + 1, BF16_ROWS)
                  if rows % r == 0 and rows // r <= n_steps)
        last = rows // rb - 1
        if layer is None:
            in_specs.append(pl.BlockSpec((rb, cols), lambda i, last=last: (jnp.minimum(i, last), 0)))
        else:
            in_specs.append(pl.BlockSpec(
                (None, rb, cols), lambda i, last=last, layer=layer: (layer, jnp.minimum(i, last), 0)))
        out_specs.append(pl.BlockSpec((rb, cols), lambda i, last=last: (jnp.minimum(i, last), 0)))
        out_shapes.append(jax.ShapeDtypeStruct((rows, cols), BF16))
    return in_specs, out_specs, out_shapes


def _run_cast_jobs(in_refs, out_refs):
    for src_ref, dst_ref in zip(in_refs, out_refs):
        dst_ref[...] = src_ref[...].astype(BF16)


def _cast_kernel(*refs):
    n = len(refs) // 2
    _run_cast_jobs(refs[:n], refs[n:])


def _cast_weights(jobs):
    in_specs, out_specs, out_shapes = _cast_job_specs(jobs, CAST_STEPS)
    return pl.pallas_call(
        _cast_kernel,
        grid=(CAST_STEPS,),
        in_specs=in_specs,
        out_specs=out_specs,
        out_shape=out_shapes,
        compiler_params=_params(1),
        name="cast_weights",
    )(*[src for src, _ in jobs])


def _ffn_kernel(x_ref, g_ref, wi_ref, wo_ref, *refs):
    n_jobs = (len(refs) - 2) // 2
    o_ref, act_ref = refs[n_jobs], refs[-1]
    _run_cast_jobs(refs[:n_jobs], refs[n_jobs + 1:2 * n_jobs + 1])
    d_ff = wo_ref.shape[0]
    h = _rms(x_ref[...], g_ref[...]).astype(BF16)
    lo = 0
    for ck in FFN_CHUNKS:
        gate = jnp.dot(h, wi_ref[:, lo:lo + ck], preferred_element_type=F32)
        up = jnp.dot(h, wi_ref[:, d_ff + lo:d_ff + lo + ck], preferred_element_type=F32)
        act_ref[:, lo:lo + ck] = (_silu(gate) * up).astype(BF16)
        lo += ck
    y = jnp.dot(act_ref[...], wo_ref[...], preferred_element_type=F32)
    o_ref[...] = x_ref[...] + FFN_RESIDUAL * y


def _ffn(x2, gain, wi, wo, cast_jobs=()):
    n, d = x2.shape
    d_ff = wo.shape[0]
    assert sum(FFN_CHUNKS) == d_ff and n % TM_FFN == 0
    n_steps = n // TM_FFN
    job_in, job_out, job_shapes = _cast_job_specs(cast_jobs, n_steps)
    outs = pl.pallas_call(
        _ffn_kernel,
        grid=(n_steps,),
        in_specs=[
            pl.BlockSpec((TM_FFN, d), lambda i: (i, 0)),
            _resident((1, d)),
            _resident((d, 2 * d_ff)),
            _resident((d_ff, d)),
        ] + job_in,
        out_specs=[pl.BlockSpec((TM_FFN, d), lambda i: (i, 0))] + job_out,
        out_shape=[jax.ShapeDtypeStruct((n, d), F32)] + job_shapes,
        scratch_shapes=[pltpu.VMEM((TM_FFN, d_ff), BF16)],
        compiler_params=_params(1),
        name="ffn",
    )(x2, gain.reshape(1, d), wi, wo, *[src for src, _ in cast_jobs])
    return outs[0], outs[1:]


def _xattn_kernel(x_ref, g_ref, wq_ref, qn_ref, kt_ref, v_ref, wo_ref, o_ref, att_ref):
    d = x_ref.shape[-1]
    hd = d // MEM_HEADS
    h = _rms(x_ref[0], g_ref[...]).astype(BF16)
    q = jnp.dot(h, wq_ref[...], preferred_element_type=F32)
    for i in range(MEM_HEADS):
        cols = slice(i * hd, (i + 1) * hd)
        qh = (_rms(q[:, cols], qn_ref[...]) * (hd ** -0.5)).astype(BF16)
        s = jnp.dot(qh, kt_ref[0, i], preferred_element_type=F32)
        p = jnp.exp(s - jnp.max(s, axis=-1, keepdims=True))
        inv = 1.0 / jnp.sum(p, axis=-1, keepdims=True)
        o = jnp.dot(p.astype(BF16), v_ref[0, :, cols], preferred_element_type=F32)
        att_ref[:, cols] = (o * inv).astype(BF16)
    y = jnp.dot(att_ref[...], wo_ref[...], preferred_element_type=F32)
    o_ref[0] = x_ref[0] + y


def _xattn(x3, gain, wq, qnorm, mem_kt, mem_v, wo):
    b, t, d = x3.shape
    hd = d // MEM_HEADS
    m = mem_v.shape[1]
    tm = TM_XATTN
    assert t % tm == 0
    return pl.pallas_call(
        _xattn_kernel,
        grid=(b, t // tm),
        in_specs=[
            pl.BlockSpec((1, tm, d), lambda i, j: (i, j, 0)),
            _resident((1, d)),
            _resident((d, d)),
            _resident((1, hd)),
            pl.BlockSpec((1, MEM_HEADS, hd, m), lambda i, j: (i, 0, 0, 0)),
            pl.BlockSpec((1, m, d), lambda i, j: (i, 0, 0)),
            _resident((d, d)),
        ],
        out_specs=pl.BlockSpec((1, tm, d), lambda i, j: (i, j, 0)),
        out_shape=jax.ShapeDtypeStruct((b, t, d), F32),
        scratch_shapes=[pltpu.VMEM((tm, d), BF16)],
        compiler_params=_params(2),
        name="xattn",
    )(x3, gain.reshape(1, d), wq, qnorm.reshape(1, hd), mem_kt, mem_v, wo)


CONV_TILES = CONV_CH // LANES
POOL_TILES = D_POOL // LANES
TILES_PER_POOL_GROUP = POOL_GROUP // LANES


def _even_kernel(x_ref, g_ref, wzx_ref, wpool_ref, wdt_ref, cw_ref, cb_ref, dtb_ref, alog_ref,
                 dskip_ref, norm_ref,
                 pw_ref, ps_ref, wout_ref, exph_ref, tril_ref, o_ref,
                 conv_scr, act_scr, dt_scr, z_scr, pool_scr, state_scr, conv_halo, pool_halo):
    t = pl.program_id(1)
    tm = x_ref.shape[1]
    n_chunks = tm // CHUNK

    slot = lax.rem(t, 2)

    @pl.when(t == 0)
    def _():
        conv_halo[0] = jnp.zeros((CONV_TILES, CONV_HALO, LANES), F32)
        pool_halo[0] = jnp.zeros((POOL_TILES, POOL_HALO, LANES), F32)
        state_scr[...] = jnp.zeros(state_scr.shape, F32)

    conv_scr[:, 0:CONV_HALO, :] = conv_halo[slot]
    pool_scr[:, 0:POOL_HALO, :] = pool_halo[slot]

    x = x_ref[0]
    h = _rms(x, g_ref[...]).astype(BF16)
    xbc = jnp.dot(h, wzx_ref[:, D_SSM:], preferred_element_type=F32)
    for c in range(CONV_TILES):
        conv_scr[c, CONV_HALO:CONV_HALO + tm, :] = xbc[:, c * LANES:(c + 1) * LANES]
    u_pool = jnp.dot(h, wpool_ref[...], preferred_element_type=F32)
    for c in range(POOL_TILES):
        pool_scr[c, POOL_HALO:POOL_HALO + tm, :] = u_pool[:, c * LANES:(c + 1) * LANES]
    dt_raw = jnp.dot(h, wdt_ref[...], preferred_element_type=F32) + dtb_ref[...]
    dt_scr[...] = jnp.maximum(dt_raw, 0.0) + jnp.log1p(jnp.exp(-jnp.abs(dt_raw)))
    z_scr[...] = jnp.dot(h, wzx_ref[:, 0:D_SSM], preferred_element_type=F32)

    for c in range(CONV_TILES):
        cols = slice(c * LANES, (c + 1) * LANES)
        acc = cb_ref[:, cols]
        for k in range(CONV_WIDTH):
            acc = acc + (conv_scr[c, pl.ds(CONV_HALO - CONV_WIDTH + 1 + k, tm), :]
                         * cw_ref[k:k + 1, cols])
        act_scr[:, cols] = _silu(acc)
    conv_halo[1 - slot] = conv_scr[:, tm:tm + CONV_HALO, :]

    pos = t * tm + lax.broadcasted_iota(jnp.int32, (tm, 1), 0)
    y_pool = []
    for k, w in enumerate(POOL_WINDOWS):
        inv_count = 1.0 / jnp.minimum(pos + 1, w).astype(F32)
        pooled = []
        for c in range(k * TILES_PER_POOL_GROUP, (k + 1) * TILES_PER_POOL_GROUP):
            if w > SUBLANES:
                e = pool_scr[c, pl.ds(POOL_HALO - SUBLANES, tm + SUBLANES), :]
                for i in range(1, SUBLANES):
                    e = e + pool_scr[c, pl.ds(POOL_HALO - SUBLANES - i, tm + SUBLANES), :]
                s = e[SUBLANES:, :] + e[:tm, :]
            else:
                s = pool_scr[c, pl.ds(POOL_HALO, tm), :]
                for i in range(1, w):
                    s = s + pool_scr[c, pl.ds(POOL_HALO - i, tm), :]
            pooled.append(s * inv_count - pool_scr[c, pl.ds(POOL_HALO, tm), :])
        pooled = jnp.concatenate(pooled, axis=1).astype(BF16)
        kcols = slice(k * POOL_GROUP, (k + 1) * POOL_GROUP)
        yk = jnp.dot(pooled, pw_ref[k], preferred_element_type=F32)
        y_pool.append((yk * ps_ref[:, kcols]).astype(BF16))
    pool_halo[1 - slot] = pool_scr[:, tm:tm + POOL_HALO, :]

    a_row = -jnp.exp(alog_ref[...])
    low_half = (lax.broadcasted_iota(jnp.int32, (CHUNK, D_SSM), 1) & HALF) == 0
    causal = (lax.broadcasted_iota(jnp.int32, (CHUNK, CHUNK), 0)
              >= lax.broadcasted_iota(jnp.int32, (CHUNK, CHUNK), 1))
    heads_per_group = SSD_HEADS // SSD_GROUPS

    def chunk_body(c):
        rows = pl.ds(c * CHUNK, CHUNK)
        xs = act_scr[rows, 0:D_SSM]
        dt = dt_scr[rows, :]
        adt = dt * a_row
        hi = adt.astype(BF16)
        r1 = adt - hi.astype(F32)
        mid = r1.astype(BF16)
        lo = (r1 - mid.astype(F32)).astype(BF16)
        acs = jnp.dot(tril_ref[...], jnp.concatenate([hi, mid, lo], axis=0),
                      preferred_element_type=F32)
        acs_last = acs[CHUNK - 1:CHUNK, :]
        stacked = jnp.concatenate([dt, jnp.exp(acs_last - acs), jnp.exp(acs)], axis=0)
        s_hi, s_lo = _split2(stacked)
        expanded = jnp.dot(jnp.concatenate([s_hi, s_lo], axis=1), exph_ref[...],
                           preferred_element_type=F32)
        dt_exp = expanded[0:CHUNK]
        dte_exp = expanded[CHUNK:2 * CHUNK]
        dfs_exp = expanded[2 * CHUNK:3 * CHUNK]
        xdt = xs * dt_exp
        xw_b = (xdt * dte_exp).astype(BF16)
        x_even = jnp.where(low_half, xdt, 0.0).astype(BF16)
        x_odd = jnp.where(low_half, 0.0, xdt).astype(BF16)
        acs_row = acs.T

        y_tiles = []
        for g in range(SSD_GROUPS):
            gcols = slice(g * GROUP_W, (g + 1) * GROUP_W)
            b_g = act_scr[rows, D_SSM + g * D_STATE:D_SSM + (g + 1) * D_STATE]
            c_g = act_scr[rows, D_SSM + SSD_GROUPS * D_STATE + g * D_STATE:
                          D_SSM + SSD_GROUPS * D_STATE + (g + 1) * D_STATE].astype(BF16)
            bt_g = b_g.T.astype(BF16)
            cb = jnp.dot(c_g, bt_g, preferred_element_type=F32)
            s_new = jnp.dot(bt_g, xw_b[:, gcols], preferred_element_type=F32)
            s_in = state_scr[g]
            y_off = jnp.dot(c_g, s_in.astype(BF16), preferred_element_type=F32) * dfs_exp[:, gcols]
            state_scr[g] = s_in * dfs_exp[CHUNK - 1:CHUNK, gcols] + s_new
            for pair in range(heads_per_group // 2):
                pcols = slice(g * GROUP_W + pair * LANES, g * GROUP_W + (pair + 1) * LANES)
                w2 = []
                for parity in range(2):
                    hh = g * heads_per_group + 2 * pair + parity
                    seg = jnp.where(causal, jnp.exp(acs[:, hh:hh + 1] - acs_row[hh:hh + 1, :]), 0.0)
                    w2.append((cb * seg).astype(BF16))
                y_diag = jnp.dot(jnp.concatenate(w2, axis=1),
                                 jnp.concatenate([x_even[:, pcols], x_odd[:, pcols]], axis=0),
                                 preferred_element_type=F32)
                y_tiles.append(y_diag + y_off[:, pair * LANES:(pair + 1) * LANES])
        y = (jnp.concatenate(y_tiles, axis=1) + dskip_ref[...] * xs) * _silu(z_scr[rows, :])
        return jnp.concatenate(
            [_rms(y[:, g * GROUP_W:(g + 1) * GROUP_W], norm_ref[:, g * GROUP_W:(g + 1) * GROUP_W])
             for g in range(SSD_GROUPS)], axis=1).astype(BF16)

    y_ssd = jnp.concatenate([chunk_body(c) for c in range(n_chunks)], axis=0)
    o_ref[0] = x + jnp.dot(jnp.concatenate([y_ssd] + y_pool, axis=1), wout_ref[...],
                           preferred_element_type=F32)


def _head_expand_matrix():
    m = np.zeros((2 * LANES, D_SSM), np.float32)
    for hh in range(SSD_HEADS):
        m[hh, hh * SSD_HEAD_DIM:(hh + 1) * SSD_HEAD_DIM] = 1.0
        m[LANES + hh, hh * SSD_HEAD_DIM:(hh + 1) * SSD_HEAD_DIM] = 1.0
    return jnp.asarray(m, BF16)


def _even_mixer(x3, gain, in_proj, conv_w, conv_b, dt_bias, a_log, d_skip, ssd_norm,
                pool_w, pool_scale, out_proj):
    b, t, d = x3.shape
    tm = TM_EVEN
    assert t % tm == 0 and tm % CHUNK == 0
    o_dt = D_SSM + CONV_CH
    o_pool = o_dt + SSD_HEADS
    pad = LANES - SSD_HEADS
    w_bf = in_proj.astype(BF16)
    w_zx, w_pool = w_bf[:, :o_dt], w_bf[:, o_pool:]
    w_dt = jnp.pad(w_bf[:, o_dt:o_pool], ((0, 0), (0, pad)))
    dtb = jnp.pad(dt_bias, (0, pad)).reshape(1, LANES)
    alog = jnp.pad(a_log, (0, pad)).reshape(1, LANES)
    dskip = jnp.repeat(d_skip, SSD_HEAD_DIM).reshape(1, D_SSM)
    tril3 = jnp.asarray(np.tile(np.tril(np.ones((CHUNK, CHUNK), np.float32)), (1, 3)), BF16)
    n_g = len(POOL_WINDOWS)
    return pl.pallas_call(
        _even_kernel,
        grid=(b, t // tm),
        in_specs=[
            pl.BlockSpec((1, tm, d), lambda i, j: (i, j, 0)),
            _resident((1, d)),
            _resident((d, o_dt)),
            _resident((d, D_POOL)),
            _resident((d, LANES)),
            _resident((CONV_WIDTH, CONV_CH)),
            _resident((1, CONV_CH)),
            _resident((1, LANES)),
            _resident((1, LANES)),
            _resident((1, D_SSM)),
            _resident((1, D_SSM)),
            _resident((n_g, POOL_GROUP, POOL_GROUP)),
            _resident((1, D_POOL)),
            _resident((D_SSM + D_POOL, d)),
            _resident((2 * LANES, D_SSM)),
            _resident((CHUNK, 3 * CHUNK)),
        ],
        out_specs=pl.BlockSpec((1, tm, d), lambda i, j: (i, j, 0)),
        out_shape=jax.ShapeDtypeStruct((b, t, d), F32),
        scratch_shapes=[
            pltpu.VMEM((CONV_TILES, tm + CONV_HALO, LANES), F32),
            pltpu.VMEM((tm, CONV_CH), F32),
            pltpu.VMEM((tm, LANES), F32),
            pltpu.VMEM((tm, D_SSM), F32),
            pltpu.VMEM((POOL_TILES, tm + POOL_HALO, LANES), F32),
            pltpu.VMEM((SSD_GROUPS, D_STATE, GROUP_W), F32),
            pltpu.VMEM((2, CONV_TILES, CONV_HALO, LANES), F32),
            pltpu.VMEM((2, POOL_TILES, POOL_HALO, LANES), F32),
        ],
        compiler_params=_params(2),
        name="even_mixer",
    )(x3, gain.reshape(1, d), w_zx, w_pool, w_dt, conv_w, conv_b.reshape(1, CONV_CH), dtb, alog,
      dskip,
      ssd_norm.reshape(1, D_SSM), pool_w.reshape(n_g, POOL_GROUP, POOL_GROUP),
      pool_scale.reshape(1, D_POOL), out_proj, _head_expand_matrix(), tril3)


N_QK_HEADS = ATTN_HEADS + ATTN_KV_HEADS
QK_W = N_QK_HEADS * ATTN_HEAD_DIM
Q_W = ATTN_HEADS * ATTN_HEAD_DIM
KV_W = ATTN_KV_HEADS * ATTN_HEAD_DIM
KV_PAIRS = ATTN_KV_HEADS // 2
STACK = ATTN_GROUP * BLOCK
WO_BLOCKS = 4


def _odd_kernel(sink_ref, x_ref, g_ref, wqkv_ref, bqkv_ref, gqk_ref, headmean_ref, bias_ref,
                wo_ref, bo_ref, o_ref, q_scr, k_scr, v_scr, k_prev, v_prev):
    t = pl.program_id(1)
    tm = x_ref.shape[1]
    n_blocks = tm // BLOCK
    n_forms = 2 * KV_PAIRS

    slot = lax.rem(t, 2)

    @pl.when(t == 0)
    def _():
        k_prev[0] = jnp.zeros((n_forms, BLOCK, LANES), BF16)
        v_prev[0] = jnp.zeros((n_forms, BLOCK, 2 * LANES), BF16)

    k_scr[:, 0:BLOCK, :] = k_prev[slot]
    v_scr[:, 0:BLOCK, :] = v_prev[slot]

    x = x_ref[0]
    h = _rms(x, g_ref[...]).astype(BF16)
    qkv = jnp.dot(h, wqkv_ref[...], preferred_element_type=F32) + bqkv_ref[...]
    qk = qkv[:, 0:QK_W]
    sq = (qk * qk).astype(BF16)
    msq = jnp.concatenate(
        [jnp.dot(sq[:, c0:c0 + MXU_WIDTH], headmean_ref[...], preferred_element_type=F32)
         for c0 in range(0, QK_W, MXU_WIDTH)], axis=1)
    qkn = qk * lax.rsqrt(msq + EPS) * gqk_ref[...]
    q_scr[...] = qkn[:, 0:Q_W].astype(BF16)

    low_half = lax.broadcasted_iota(jnp.int32, (tm, LANES), 1) < HALF
    zeros = jnp.zeros((tm, LANES), F32)
    ones = (jnp.where(low_half, 1.0, 0.0).astype(BF16), jnp.where(low_half, 0.0, 1.0).astype(BF16))
    for is_v, src, dst, base in ((False, qkn, k_scr, Q_W), (True, qkv, v_scr, QK_W)):
        for kp in range(KV_PAIRS):
            a2 = src[:, base + kp * LANES:base + (kp + 1) * LANES]
            for f, val in enumerate((jnp.where(low_half, a2, zeros), jnp.where(low_half, zeros, a2))):
                dst[2 * kp + f, BLOCK:BLOCK + tm, 0:LANES] = val.astype(BF16)
                if is_v:
                    dst[2 * kp + f, BLOCK:BLOCK + tm, LANES:2 * LANES] = ones[f]

    nt_dims = (((1,), (1,)), ((), ()))
    row = lax.broadcasted_iota(jnp.int32, (STACK, LANES), 0)
    col = lax.broadcasted_iota(jnp.int32, (STACK, LANES), 1)
    tri = col <= (row & (BLOCK - 1))
    lane_low = col < HALF
    row_group = lax.broadcasted_iota(jnp.int32, (STACK, 1), 0) // BLOCK

    def sink_column(kvh):
        sink = jnp.full((STACK, 1), sink_ref[kvh * ATTN_GROUP], F32)
        for g in range(1, ATTN_GROUP):
            sink = jnp.where(row_group == g, sink_ref[kvh * ATTN_GROUP + g], sink)
        return sink

    def block_attention(i):
        rows = pl.ds(i * BLOCK, BLOCK)
        krows = pl.ds(i * BLOCK, 2 * BLOCK)
        sel = jnp.where(t == 0, 1, 0) if i == 0 else 0
        att_tiles = []
        for kp in range(KV_PAIRS):
            col0 = kp * ATTN_GROUP * LANES
            q4 = jnp.concatenate(
                [q_scr[rows, col0 + g * LANES:col0 + (g + 1) * LANES] for g in range(ATTN_GROUP)],
                axis=0)
            acc = None
            shifts = []
            for f in range(2):
                s2 = lax.dot_general(q4, k_scr[2 * kp + f, krows, :], nt_dims,
                                     preferred_element_type=F32)
                sc = jnp.where(tri, s2[:, BLOCK:], s2[:, :BLOCK]) + bias_ref[sel, kp, f]
                sink = sink_column(2 * kp + f)
                m = jnp.maximum(jnp.max(sc, axis=-1, keepdims=True), sink)
                p = jnp.exp(sc - m)
                pcat = jnp.concatenate([jnp.where(tri, 0.0, p), jnp.where(tri, p, 0.0)],
                                       axis=1).astype(BF16)
                part = jnp.dot(pcat, v_scr[2 * kp + f, krows, :],
                               preferred_element_type=F32)
                acc = part if acc is None else acc + part
                shifts.append(sink - m)
            denom = acc[:, LANES:] + jnp.exp(jnp.where(lane_low, shifts[0], shifts[1]))
            att = (acc[:, :LANES] * (1.0 / denom)).astype(BF16)
            att_tiles.extend(att[g * BLOCK:(g + 1) * BLOCK] for g in range(ATTN_GROUP))
        return jnp.concatenate(att_tiles, axis=1)

    for i0 in range(0, n_blocks, WO_BLOCKS):
        att = jnp.concatenate([block_attention(i) for i in range(i0, i0 + WO_BLOCKS)], axis=0)
        rows = pl.ds(i0 * BLOCK, WO_BLOCKS * BLOCK)
        o_ref[0, rows, :] = (x_ref[0, rows, :] + bo_ref[...]
                             + jnp.dot(att, wo_ref[...], preferred_element_type=F32))
    k_prev[1 - slot] = k_scr[:, tm:tm + BLOCK, :]
    v_prev[1 - slot] = v_scr[:, tm:tm + BLOCK, :]


def _alibi_bias():
    slopes = 2.0 ** (-8.0 * (np.arange(ATTN_HEADS) + 1) / ATTN_HEADS)
    q = np.arange(BLOCK)[:, None]
    j = np.arange(BLOCK)[None, :]
    cur = j <= q
    dist = np.where(cur, q - j, q - j + BLOCK).astype(np.float32)
    assert dist.min() >= 0 and dist.max() < WINDOW
    out = np.zeros((2, KV_PAIRS, 2, STACK, BLOCK), np.float32)
    for kp in range(KV_PAIRS):
        for f in range(2):
            for g in range(ATTN_GROUP):
                head = (2 * kp + f) * ATTN_GROUP + g
                base = -np.float32(slopes[head]) * dist
                out[0, kp, f, g * BLOCK:(g + 1) * BLOCK] = base
                out[1, kp, f, g * BLOCK:(g + 1) * BLOCK] = np.where(cur, base, -np.inf)
    return jnp.asarray(out)


def _to_kernel_head_order(a, axis):
    shape = a.shape
    split = shape[:axis] + (KV_PAIRS, 2, ATTN_GROUP, ATTN_HEAD_DIM) + shape[axis + 1:]
    return jnp.swapaxes(a.reshape(split), axis + 1, axis + 2).reshape(shape)


def _head_mean_matrix():
    idx = np.arange(MXU_WIDTH) // ATTN_HEAD_DIM
    return jnp.asarray((idx[:, None] == idx[None, :]).astype(np.float32) / ATTN_HEAD_DIM, BF16)


def _odd_mixer(x3, gain, wqkv, bqkv, qnorm, knorm, sinks, wo, bo):
    b, t, d = x3.shape
    tm = TM_ODD
    assert t % tm == 0 and tm % (WO_BLOCKS * BLOCK) == 0 and QK_W % MXU_WIDTH == 0
    qkv_w = wqkv.shape[1]
    gqk = jnp.concatenate([jnp.tile(qnorm, ATTN_HEADS) * (ATTN_HEAD_DIM ** -0.5),
                           jnp.tile(knorm, ATTN_KV_HEADS)]).reshape(1, QK_W)
    wqkv = jnp.concatenate([_to_kernel_head_order(wqkv[:, :Q_W], 1), wqkv[:, Q_W:]], axis=1)
    bqkv = jnp.concatenate([_to_kernel_head_order(bqkv[:Q_W], 0), bqkv[Q_W:]])
    wo = _to_kernel_head_order(wo, 0)
    n_forms = 2 * KV_PAIRS
    return pl.pallas_call(
        _odd_kernel,
        grid=(b, t // tm),
        in_specs=[
            pl.BlockSpec(memory_space=pltpu.SMEM),
            pl.BlockSpec((1, tm, d), lambda i, j: (i, j, 0)),
            _resident((1, d)),
            _resident((d, qkv_w)),
            _resident((1, qkv_w)),
            _resident((1, QK_W)),
            _resident((MXU_WIDTH, MXU_WIDTH)),
            _resident((2, KV_PAIRS, 2, STACK, BLOCK)),
            _resident((Q_W, d)),
            _resident((1, d)),
        ],
        out_specs=pl.BlockSpec((1, tm, d), lambda i, j: (i, j, 0)),
        out_shape=jax.ShapeDtypeStruct((b, t, d), F32),
        scratch_shapes=[
            pltpu.VMEM((tm, Q_W), BF16),
            pltpu.VMEM((n_forms, tm + BLOCK, LANES), BF16),
            pltpu.VMEM((n_forms, tm + BLOCK, 2 * LANES), BF16),
            pltpu.VMEM((2, n_forms, BLOCK, LANES), BF16),
            pltpu.VMEM((2, n_forms, BLOCK, 2 * LANES), BF16),
        ],
        compiler_params=_params(2),
        name="odd_mixer",
    )(sinks, x3, gain.reshape(1, d), wqkv, bqkv.reshape(1, qkv_w), gqk,
      _head_mean_matrix(), _alibi_bias(), wo, bo.reshape(1, d))


def kernel(x, mem, mem_norm, mem_wkv, mem_knorm, ffn1_norm, ffn1_wi, ffn1_wo, mix_norm, ssd_in_proj, ssd_conv_w, ssd_conv_b, ssd_dt_bias, ssd_a_log, ssd_d, ssd_norm, pool_w, pool_scale, even_out_proj, attn_wqkv, attn_bqkv, attn_qnorm, attn_knorm, attn_sinks, attn_wo, attn_bo, xattn_norm, xattn_wq, xattn_qnorm, xattn_wo, ffn2_norm, ffn2_wi, ffn2_wo):
    b, t, d = x.shape
    depth = ffn1_norm.shape[0]
    mem_kt, mem_v = _mem_kv(mem, mem_norm, mem_wkv, mem_knorm)
    pool_w2 = pool_w.reshape(pool_w.shape[0], D_POOL, POOL_GROUP)
    wi1, wo1 = _cast_weights([(ffn1_wi, 0), (ffn1_wo, 0)])
    for i in range(depth):
        jobs = [(ffn2_wi, i), (ffn2_wo, i), (xattn_wq, i), (xattn_wo, i)]
        if i % 2 == 0:
            jobs += [(pool_w2, i // 2), (even_out_proj, i // 2)]
        else:
            jobs += [(attn_wqkv, i // 2), (attn_wo, i // 2)]
        x2, (wi2, wo2, xq, xo, mix_a, mix_b) = _ffn(x.reshape(b * t, d), ffn1_norm[i], wi1, wo1, jobs)
        x = x2.reshape(b, t, d)
        if i % 2 == 0:
            e = i // 2
            x = _even_mixer(x, mix_norm[i], ssd_in_proj[e], ssd_conv_w[e], ssd_conv_b[e],
                            ssd_dt_bias[e], ssd_a_log[e], ssd_d[e], ssd_norm[e], mix_a,
                            pool_scale[e], mix_b)
        else:
            o = i // 2
            x = _odd_mixer(x, mix_norm[i], mix_a, attn_bqkv[o], attn_qnorm[o], attn_knorm[o],
                           attn_sinks[o], mix_b, attn_bo[o])
        x = _xattn(x, xattn_norm[i], xq, xattn_qnorm[i], mem_kt, mem_v, xo)
        jobs = [(ffn1_wi, i + 1), (ffn1_wo, i + 1)] if i + 1 < depth else []
        x2, nxt = _ffn(x.reshape(b * t, d), ffn2_norm[i], wi2, wo2, jobs)
        x = x2.reshape(b, t, d)
        if nxt:
            wi1, wo1 = nxt
    return x
```

```python
import numpy as np
import jax
import jax.numpy as jnp
from jax import lax
from jax.experimental import pallas as pl
from jax.experimental.pallas import tpu as pltpu

F32 = jnp.float32
BF16 = jnp.bfloat16

EPS = 1e-6
FFN_RESIDUAL = 0.5
SSD_HEADS = 16
SSD_HEAD_DIM = 64
SSD_GROUPS = 2
D_STATE = 128
CONV_WIDTH = 4
CHUNK = 128
D_SSM = SSD_HEADS * SSD_HEAD_DIM
GROUP_W = D_SSM // SSD_GROUPS
CONV_CH = D_SSM + 2 * SSD_GROUPS * D_STATE
POOL_WINDOWS = (2, 4, 8, 16)
POOL_GROUP = 256
D_POOL = POOL_GROUP * len(POOL_WINDOWS)
POOL_HALO = 16
CONV_HALO = 8
ATTN_HEADS = 16
ATTN_KV_HEADS = 4
ATTN_GROUP = ATTN_HEADS // ATTN_KV_HEADS
ATTN_HEAD_DIM = 64
WINDOW = 128
BLOCK = 128
MEM_HEADS = 4

LANES = 128
SUBLANES = 8
HALF = 64
MXU_WIDTH = 256
BF16_ROWS = 16
CAST_STEPS = 8

TM_FFN = 1024
TM_XATTN = 2048
TM_EVEN = 1024
TM_ODD = 1024
VMEM_LIMIT_BYTES = 56 * 1024 * 1024
FFN_CHUNKS = (1024, 1024, 768)


def _rms(x, gain):
    ms = jnp.mean(x * x, axis=-1, keepdims=True)
    return x * lax.rsqrt(ms + EPS) * gain


def _silu(x):
    return x * (1.0 / (1.0 + jnp.exp(-x)))


def _split2(x):
    hi = x.astype(BF16)
    lo = (x - hi.astype(F32)).astype(BF16)
    return hi, lo


def _resident(shape):
    nd = len(shape)
    return pl.BlockSpec(shape, lambda *_: (0,) * nd, pipeline_mode=pl.Buffered(1))


def _params(n_axes):
    return pltpu.CompilerParams(
        dimension_semantics=("arbitrary",) * n_axes,
        vmem_limit_bytes=VMEM_LIMIT_BYTES,
    )


def _memkv_kernel(mem_ref, g_ref, wkv_ref, kn_ref, kt_ref, v_ref):
    d = mem_ref.shape[-1]
    hd = d // MEM_HEADS
    h = _rms(mem_ref[0], g_ref[...]).astype(BF16)
    kv = jnp.dot(h, wkv_ref[...], preferred_element_type=F32)
    for i in range(MEM_HEADS):
        kh = _rms(kv[:, i * hd:(i + 1) * hd], kn_ref[...])
        kt_ref[0, i] = kh.T.astype(BF16)
    v_ref[0] = kv[:, d:].astype(BF16)


def _mem_kv(mem, mem_norm, wkv, mem_knorm):
    b, m, d = mem.shape
    hd = d // MEM_HEADS
    return pl.pallas_call(
        _memkv_kernel,
        grid=(b,),
        in_specs=[
            pl.BlockSpec((1, m, d), lambda i: (i, 0, 0)),
            _resident((1, d)),
            _resident((d, 2 * d)),
            _resident((1, hd)),
        ],
        out_specs=[
            pl.BlockSpec((1, MEM_HEADS, hd, m), lambda i: (i, 0, 0, 0)),
            pl.BlockSpec((1, m, d), lambda i: (i, 0, 0)),
        ],
        out_shape=[
            jax.ShapeDtypeStruct((b, MEM_HEADS, hd, m), BF16),
            jax.ShapeDtypeStruct((b, m, d), BF16),
        ],
        compiler_params=_params(1),
        name="mem_kv",
    )(mem, mem_norm.reshape(1, d), wkv.astype(BF16), mem_knorm.reshape(1, hd))


def _cast_job_specs(jobs, n_steps):
    in_specs, out_specs, out_shapes = [], [], []
    for src, layer in jobs:
        rows, cols = src.shape[-2:]
        rb = next(r for r in range(BF16_ROWS, rows + 1, BF16_ROWS)
                  if rows % r == 0 and rows // r <= n_steps)
        last = rows // rb - 1
        if layer is None:
            in_specs.append(pl.BlockSpec((rb, cols), lambda i, last=last: (jnp.minimum(i, last), 0)))
        else:
            in_specs.append(pl.BlockSpec(
                (None, rb, cols), lambda i, last=last, layer=layer: (layer, jnp.minimum(i, last), 0)))
        out_specs.append(pl.BlockSpec((rb, cols), lambda i, last=last: (jnp.minimum(i, last), 0)))
        out_shapes.append(jax.ShapeDtypeStruct((rows, cols), BF16))
    return in_specs, out_specs, out_shapes


def _run_cast_jobs(in_refs, out_refs):
    for src_ref, dst_ref in zip(in_refs, out_refs):
        dst_ref[...] = src_ref[...].astype(BF16)


def _cast_kernel(*refs):
    n = len(refs) // 2
    _run_cast_jobs(refs[:n], refs[n:])


def _cast_weights(jobs):
    in_specs, out_specs, out_shapes = _cast_job_specs(jobs, CAST_STEPS)
    return pl.pallas_call(
        _cast_kernel,
        grid=(CAST_STEPS,),
        in_specs=in_specs,
        out_specs=out_specs,
        out_shape=out_shapes,
        compiler_params=_params(1),
        name="cast_weights",
    )(*[src for src, _ in jobs])


def _ffn_kernel(x_ref, g_ref, wi_ref, wo_ref, *refs):
    n_jobs = (len(refs) - 2) // 2
    o_ref, act_ref = refs[n_jobs], refs[-1]
    _run_cast_jobs(refs[:n_jobs], refs[n_jobs + 1:2 * n_jobs + 1])
    d_ff = wo_ref.shape[0]
    h = _rms(x_ref[...], g_ref[...]).astype(BF16)
    lo = 0
    for ck in FFN_CHUNKS:
        gate = jnp.dot(h, wi_ref[:, lo:lo + ck], preferred_element_type=F32)
        up = jnp.dot(h, wi_ref[:, d_ff + lo:d_ff + lo + ck], preferred_element_type=F32)
        act_ref[:, lo:lo + ck] = (_silu(gate) * up).astype(BF16)
        lo += ck
    y = jnp.dot(act_ref[...], wo_ref[...], preferred_element_type=F32)
    o_ref[...] = x_ref[...] + FFN_RESIDUAL * y


def _ffn(x2, gain, wi, wo, cast_jobs=()):
    n, d = x2.shape
    d_ff = wo.shape[0]
    assert sum(FFN_CHUNKS) == d_ff and n % TM_FFN == 0
    n_steps = n // TM_FFN
    job_in, job_out, job_shapes = _cast_job_specs(cast_jobs, n_steps)
    outs = pl.pallas_call(
        _ffn_kernel,
        grid=(n_steps,),
        in_specs=[
            pl.BlockSpec((TM_FFN, d), lambda i: (i, 0)),
            _resident((1, d)),
            _resident((d, 2 * d_ff)),
            _resident((d_ff, d)),
        ] + job_in,
        out_specs=[pl.BlockSpec((TM_FFN, d), lambda i: (i, 0))] + job_out,
        out_shape=[jax.ShapeDtypeStruct((n, d), F32)] + job_shapes,
        scratch_shapes=[pltpu.VMEM((TM_FFN, d_ff), BF16)],
        compiler_params=_params(1),
        name="ffn",
    )(x2, gain.reshape(1, d), wi, wo, *[src for src, _ in cast_jobs])
    return outs[0], outs[1:]


def _xattn_kernel(x_ref, g_ref, wq_ref, qn_ref, kt_ref, v_ref, wo_ref, o_ref, att_ref):
    d = x_ref.shape[-1]
    hd = d // MEM_HEADS
    h = _rms(x_ref[0], g_ref[...]).astype(BF16)
    q = jnp.dot(h, wq_ref[...], preferred_element_type=F32)
    for i in range(MEM_HEADS):
        cols = slice(i * hd, (i + 1) * hd)
        qh = (_rms(q[:, cols], qn_ref[...]) * (hd ** -0.5)).astype(BF16)
        s = jnp.dot(qh, kt_ref[0, i], preferred_element_type=F32)
        p = jnp.exp(s - jnp.max(s, axis=-1, keepdims=True))
        inv = 1.0 / jnp.sum(p, axis=-1, keepdims=True)
        o = jnp.dot(p.astype(BF16), v_ref[0, :, cols], preferred_element_type=F32)
        att_ref[:, cols] = (o * inv).astype(BF16)
    y = jnp.dot(att_ref[...], wo_ref[...], preferred_element_type=F32)
    o_ref[0] = x_ref[0] + y


def _xattn(x3, gain, wq, qnorm, mem_kt, mem_v, wo):
    b, t, d = x3.shape
    hd = d // MEM_HEADS
    m = mem_v.shape[1]
    tm = TM_XATTN
    assert t % tm == 0
    return pl.pallas_call(
        _xattn_kernel,
        grid=(b, t // tm),
        in_specs=[
            pl.BlockSpec((1, tm, d), lambda i, j: (i, j, 0)),
            _resident((1, d)),
            _resident((d, d)),
            _resident((1, hd)),
            pl.BlockSpec((1, MEM_HEADS, hd, m), lambda i, j: (i, 0, 0, 0)),
            pl.BlockSpec((1, m, d), lambda i, j: (i, 0, 0)),
            _resident((d, d)),
        ],
        out_specs=pl.BlockSpec((1, tm, d), lambda i, j: (i, j, 0)),
        out_shape=jax.ShapeDtypeStruct((b, t, d), F32),
        scratch_shapes=[pltpu.VMEM((tm, d), BF16)],
        compiler_params=_params(2),
        name="xattn",
    )(x3, gain.reshape(1, d), wq, qnorm.reshape(1, hd), mem_kt, mem_v, wo)


CONV_TILES = CONV_CH // LANES
POOL_TILES = D_POOL // LANES
TILES_PER_POOL_GROUP = POOL_GROUP // LANES


def _even_kernel(x_ref, g_ref, wzx_ref, wpool_ref, wdt_ref, cw_ref, cb_ref, dtb_ref, alog_ref,
                 dskip_ref, norm_ref,
                 pw_ref, ps_ref, wout_ref, exph_ref, tril_ref, o_ref,
                 conv_scr, act_scr, dt_scr, z_scr, pool_scr, state_scr, conv_halo, pool_halo):
    t = pl.program_id(1)
    tm = x_ref.shape[1]
    n_chunks = tm // CHUNK

    slot = lax.rem(t, 2)

    @pl.when(t == 0)
    def _():
        conv_halo[0] = jnp.zeros((CONV_TILES, CONV_HALO, LANES), F32)
        pool_halo[0] = jnp.zeros((POOL_TILES, POOL_HALO, LANES), F32)
        state_scr[...] = jnp.zeros(state_scr.shape, F32)

    conv_scr[:, 0:CONV_HALO, :] = conv_halo[slot]
    pool_scr[:, 0:POOL_HALO, :] = pool_halo[slot]

    x = x_ref[0]
    h = _rms(x, g_ref[...]).astype(BF16)
    xbc = jnp.dot(h, wzx_ref[:, D_SSM:], preferred_element_type=F32)
    for c in range(CONV_TILES):
        conv_scr[c, CONV_HALO:CONV_HALO + tm, :] = xbc[:, c * LANES:(c + 1) * LANES]
    u_pool = jnp.dot(h, wpool_ref[...], preferred_element_type=F32)
    for c in range(POOL_TILES):
        pool_scr[c, POOL_HALO:POOL_HALO + tm, :] = u_pool[:, c * LANES:(c + 1) * LANES]
    dt_raw = jnp.dot(h, wdt_ref[...], preferred_element_type=F32) + dtb_ref[...]
    dt_scr[...] = jnp.maximum(dt_raw, 0.0) + jnp.log1p(jnp.exp(-jnp.abs(dt_raw)))
    z_scr[...] = jnp.dot(h, wzx_ref[:, 0:D_SSM], preferred_element_type=F32)

    for c in range(CONV_TILES):
        cols = slice(c * LANES, (c + 1) * LANES)
        acc = cb_ref[:, cols]
        for k in range(CONV_WIDTH):
            acc = acc + (conv_scr[c, pl.ds(CONV_HALO - CONV_WIDTH + 1 + k, tm), :]
                         * cw_ref[k:k + 1, cols])
        act_scr[:, cols] = _silu(acc)
    conv_halo[1 - slot] = conv_scr[:, tm:tm + CONV_HALO, :]

    pos = t * tm + lax.broadcasted_iota(jnp.int32, (tm, 1), 0)
    y_pool = []
    for k, w in enumerate(POOL_WINDOWS):
        inv_count = 1.0 / jnp.minimum(pos + 1, w).astype(F32)
        pooled = []
        for c in range(k * TILES_PER_POOL_GROUP, (k + 1) * TILES_PER_POOL_GROUP):
            if w > SUBLANES:
                e = pool_scr[c, pl.ds(POOL_HALO - SUBLANES, tm + SUBLANES), :]
                for i in range(1, SUBLANES):
                    e = e + pool_scr[c, pl.ds(POOL_HALO - SUBLANES - i, tm + SUBLANES), :]
                s = e[SUBLANES:, :] + e[:tm, :]
            else:
                s = pool_scr[c, pl.ds(POOL_HALO, tm), :]
                for i in range(1, w):
                    s = s + pool_scr[c, pl.ds(POOL_HALO - i, tm), :]
            pooled.append(s * inv_count - pool_scr[c, pl.ds(POOL_HALO, tm), :])
        pooled = jnp.concatenate(pooled, axis=1).astype(BF16)
        kcols = slice(k * POOL_GROUP, (k + 1) * POOL_GROUP)
        yk = jnp.dot(pooled, pw_ref[k], preferred_element_type=F32)
        y_pool.append((yk * ps_ref[:, kcols]).astype(BF16))
    pool_halo[1 - slot] = pool_scr[:, tm:tm + POOL_HALO, :]

    a_row = -jnp.exp(alog_ref[...])
    low_half = (lax.broadcasted_iota(jnp.int32, (CHUNK, D_SSM), 1) & HALF) == 0
    causal = (lax.broadcasted_iota(jnp.int32, (CHUNK, CHUNK), 0)
              >= lax.broadcasted_iota(jnp.int32, (CHUNK, CHUNK), 1))
    heads_per_group = SSD_HEADS // SSD_GROUPS

    def chunk_body(c):
        rows = pl.ds(c * CHUNK, CHUNK)
        xs = act_scr[rows, 0:D_SSM]
        dt = dt_scr[rows, :]
        adt = dt * a_row
        hi = adt.astype(BF16)
        r1 = adt - hi.astype(F32)
        mid = r1.astype(BF16)
        lo = (r1 - mid.astype(F32)).astype(BF16)
        acs = jnp.dot(tril_ref[...], jnp.concatenate([hi, mid, lo], axis=0),
                      preferred_element_type=F32)
        acs_last = acs[CHUNK - 1:CHUNK, :]
        stacked = jnp.concatenate([dt, jnp.exp(acs_last - acs), jnp.exp(acs)], axis=0)
        s_hi, s_lo = _split2(stacked)
        expanded = jnp.dot(jnp.concatenate([s_hi, s_lo], axis=1), exph_ref[...],
                           preferred_element_type=F32)
        dt_exp = expanded[0:CHUNK]
        dte_exp = expanded[CHUNK:2 * CHUNK]
        dfs_exp = expanded[2 * CHUNK:3 * CHUNK]
        xdt = xs * dt_exp
        xw_b = (xdt * dte_exp).astype(BF16)
        x_even = jnp.where(low_half, xdt, 0.0).astype(BF16)
        x_odd = jnp.where(low_half, 0.0, xdt).astype(BF16)
        acs_row = acs.T

        y_tiles = []
        for g in range(SSD_GROUPS):
            gcols = slice(g * GROUP_W, (g + 1) * GROUP_W)
            b_g = act_scr[rows, D_SSM + g * D_STATE:D_SSM + (g + 1) * D_STATE]
            c_g = act_scr[rows, D_SSM + SSD_GROUPS * D_STATE + g * D_STATE:
                          D_SSM + SSD_GROUPS * D_STATE + (g + 1) * D_STATE].astype(BF16)
            bt_g = b_g.T.astype(BF16)
            cb = jnp.dot(c_g, bt_g, preferred_element_type=F32)
            s_new = jnp.dot(bt_g, xw_b[:, gcols], preferred_element_type=F32)
            s_in = state_scr[g]
            y_off = jnp.dot(c_g, s_in.astype(BF16), preferred_element_type=F32) * dfs_exp[:, gcols]
            state_scr[g] = s_in * dfs_exp[CHUNK - 1:CHUNK, gcols] + s_new
            for pair in range(heads_per_group // 2):
                pcols = slice(g * GROUP_W + pair * LANES, g * GROUP_W + (pair + 1) * LANES)
                w2 = []
                for parity in range(2):
                    hh = g * heads_per_group + 2 * pair + parity
                    seg = jnp.where(causal, jnp.exp(acs[:, hh:hh + 1] - acs_row[hh:hh + 1, :]), 0.0)
                    w2.append((cb * seg).astype(BF16))
                y_diag = jnp.dot(jnp.concatenate(w2, axis=1),
                                 jnp.concatenate([x_even[:, pcols], x_odd[:, pcols]], axis=0),
                                 preferred_element_type=F32)
                y_tiles.append(y_diag + y_off[:, pair * LANES:(pair + 1) * LANES])
        y = (jnp.concatenate(y_tiles, axis=1) + dskip_ref[...] * xs) * _silu(z_scr[rows, :])
        return jnp.concatenate(
            [_rms(y[:, g * GROUP_W:(g + 1) * GROUP_W], norm_ref[:, g * GROUP_W:(g + 1) * GROUP_W])
             for g in range(SSD_GROUPS)], axis=1).astype(BF16)

    y_ssd = jnp.concatenate([chunk_body(c) for c in range(n_chunks)], axis=0)
    o_ref[0] = x + jnp.dot(jnp.concatenate([y_ssd] + y_pool, axis=1), wout_ref[...],
                           preferred_element_type=F32)


def _head_expand_matrix():
    m = np.zeros((2 * LANES, D_SSM), np.float32)
    for hh in range(SSD_HEADS):
        m[hh, hh * SSD_HEAD_DIM:(hh + 1) * SSD_HEAD_DIM] = 1.0
        m[LANES + hh, hh * SSD_HEAD_DIM:(hh + 1) * SSD_HEAD_DIM] = 1.0
    return jnp.asarray(m, BF16)


def _even_mixer(x3, gain, in_proj, conv_w, conv_b, dt_bias, a_log, d_skip, ssd_norm,
                pool_w, pool_scale, out_proj):
    b, t, d = x3.shape
    tm = TM_EVEN
    assert t % tm == 0 and tm % CHUNK == 0
    o_dt = D_SSM + CONV_CH
    o_pool = o_dt + SSD_HEADS
    pad = LANES - SSD_HEADS
    w_bf = in_proj.astype(BF16)
    w_zx, w_pool = w_bf[:, :o_dt], w_bf[:, o_pool:]
    w_dt = jnp.pad(w_bf[:, o_dt:o_pool], ((0, 0), (0, pad)))
    dtb = jnp.pad(dt_bias, (0, pad)).reshape(1, LANES)
    alog = jnp.pad(a_log, (0, pad)).reshape(1, LANES)
    dskip = jnp.repeat(d_skip, SSD_HEAD_DIM).reshape(1, D_SSM)
    tril3 = jnp.asarray(np.tile(np.tril(np.ones((CHUNK, CHUNK), np.float32)), (1, 3)), BF16)
    n_g = len(POOL_WINDOWS)
    return pl.pallas_call(
        _even_kernel,
        grid=(b, t // tm),
        in_specs=[
            pl.BlockSpec((1, tm, d), lambda i, j: (i, j, 0)),
            _resident((1, d)),
            _resident((d, o_dt)),
            _resident((d, D_POOL)),
            _resident((d, LANES)),
            _resident((CONV_WIDTH, CONV_CH)),
            _resident((1, CONV_CH)),
            _resident((1, LANES)),
            _resident((1, LANES)),
            _resident((1, D_SSM)),
            _resident((1, D_SSM)),
            _resident((n_g, POOL_GROUP, POOL_GROUP)),
            _resident((1, D_POOL)),
            _resident((D_SSM + D_POOL, d)),
            _resident((2 * LANES, D_SSM)),
            _resident((CHUNK, 3 * CHUNK)),
        ],
        out_specs=pl.BlockSpec((1, tm, d), lambda i, j: (i, j, 0)),
        out_shape=jax.ShapeDtypeStruct((b, t, d), F32),
        scratch_shapes=[
            pltpu.VMEM((CONV_TILES, tm + CONV_HALO, LANES), F32),
            pltpu.VMEM((tm, CONV_CH), F32),
            pltpu.VMEM((tm, LANES), F32),
            pltpu.VMEM((tm, D_SSM), F32),
            pltpu.VMEM((POOL_TILES, tm + POOL_HALO, LANES), F32),
            pltpu.VMEM((SSD_GROUPS, D_STATE, GROUP_W), F32),
            pltpu.VMEM((2, CONV_TILES, CONV_HALO, LANES), F32),
            pltpu.VMEM((2, POOL_TILES, POOL_HALO, LANES), F32),
        ],
        compiler_params=_params(2),
        name="even_mixer",
    )(x3, gain.reshape(1, d), w_zx, w_pool, w_dt, conv_w, conv_b.reshape(1, CONV_CH), dtb, alog,
      dskip,
      ssd_norm.reshape(1, D_SSM), pool_w.reshape(n_g, POOL_GROUP, POOL_GROUP),
      pool_scale.reshape(1, D_POOL), out_proj, _head_expand_matrix(), tril3)


N_QK_HEADS = ATTN_HEADS + ATTN_KV_HEADS
QK_W = N_QK_HEADS * ATTN_HEAD_DIM
Q_W = ATTN_HEADS * ATTN_HEAD_DIM
KV_W = ATTN_KV_HEADS * ATTN_HEAD_DIM
KV_PAIRS = ATTN_KV_HEADS // 2
STACK = ATTN_GROUP * BLOCK
WO_BLOCKS = 8


def _odd_kernel(sink_ref, x_ref, g_ref, wqkv_ref, bqkv_ref, gqk_ref, headmean_ref, bias_ref,
                wo_ref, bo_ref, o_ref, q_scr, k_scr, v_scr, k_prev, v_prev):
    t = pl.program_id(1)
    tm = x_ref.shape[1]
    n_blocks = tm // BLOCK
    n_forms = 2 * KV_PAIRS

    slot = lax.rem(t, 2)

    @pl.when(t == 0)
    def _():
        k_prev[0] = jnp.zeros((n_forms, BLOCK, LANES), BF16)
        v_prev[0] = jnp.zeros((n_forms, BLOCK, 2 * LANES), BF16)

    k_scr[:, 0:BLOCK, :] = k_prev[slot]
    v_scr[:, 0:BLOCK, :] = v_prev[slot]

    x = x_ref[0]
    h = _rms(x, g_ref[...]).astype(BF16)
    qkv = jnp.dot(h, wqkv_ref[...], preferred_element_type=F32) + bqkv_ref[...]
    qk = qkv[:, 0:QK_W]
    sq = (qk * qk).astype(BF16)
    msq = jnp.concatenate(
        [jnp.dot(sq[:, c0:c0 + MXU_WIDTH], headmean_ref[...], preferred_element_type=F32)
         for c0 in range(0, QK_W, MXU_WIDTH)], axis=1)
    qkn = qk * lax.rsqrt(msq + EPS) * gqk_ref[...]
    q_scr[...] = qkn[:, 0:Q_W].astype(BF16)

    low_half = lax.broadcasted_iota(jnp.int32, (tm, LANES), 1) < HALF
    zeros = jnp.zeros((tm, LANES), F32)
    ones = (jnp.where(low_half, 1.0, 0.0).astype(BF16), jnp.where(low_half, 0.0, 1.0).astype(BF16))
    for is_v, src, dst, base in ((False, qkn, k_scr, Q_W), (True, qkv, v_scr, QK_W)):
        for kp in range(KV_PAIRS):
            a2 = src[:, base + kp * LANES:base + (kp + 1) * LANES]
            for f, val in enumerate((jnp.where(low_half, a2, zeros), jnp.where(low_half, zeros, a2))):
                dst[2 * kp + f, BLOCK:BLOCK + tm, 0:LANES] = val.astype(BF16)
                if is_v:
                    dst[2 * kp + f, BLOCK:BLOCK + tm, LANES:2 * LANES] = ones[f]

    nt_dims = (((1,), (1,)), ((), ()))
    row = lax.broadcasted_iota(jnp.int32, (STACK, LANES), 0)
    col = lax.broadcasted_iota(jnp.int32, (STACK, LANES), 1)
    tri = col <= (row & (BLOCK - 1))
    lane_low = col < HALF
    row_group = lax.broadcasted_iota(jnp.int32, (STACK, 1), 0) // BLOCK

    def sink_column(kvh):
        sink = jnp.full((STACK, 1), sink_ref[kvh * ATTN_GROUP], F32)
        for g in range(1, ATTN_GROUP):
            sink = jnp.where(row_group == g, sink_ref[kvh * ATTN_GROUP + g], sink)
        return sink

    def block_attention(i):
        rows = pl.ds(i * BLOCK, BLOCK)
        krows = pl.ds(i * BLOCK, 2 * BLOCK)
        sel = jnp.where(t == 0, 1, 0) if i == 0 else 0
        att_tiles = []
        for kp in range(KV_PAIRS):
            col0 = kp * ATTN_GROUP * LANES
            q4 = jnp.concatenate(
                [q_scr[rows, col0 + g * LANES:col0 + (g + 1) * LANES] for g in range(ATTN_GROUP)],
                axis=0)
            acc = None
            shifts = []
            for f in range(2):
                s2 = lax.dot_general(q4, k_scr[2 * kp + f, krows, :], nt_dims,
                                     preferred_element_type=F32)
                sc = jnp.where(tri, s2[:, BLOCK:], s2[:, :BLOCK]) + bias_ref[sel, kp, f]
                sink = sink_column(2 * kp + f)
                m = jnp.maximum(jnp.max(sc, axis=-1, keepdims=True), sink)
                p = jnp.exp(sc - m)
                pcat = jnp.concatenate([jnp.where(tri, 0.0, p), jnp.where(tri, p, 0.0)],
                                       axis=1).astype(BF16)
                part = jnp.dot(pcat, v_scr[2 * kp + f, krows, :],
                               preferred_element_type=F32)
                acc = part if acc is None else acc + part
                shifts.append(sink - m)
            denom = acc[:, LANES:] + jnp.exp(jnp.where(lane_low, shifts[0], shifts[1]))
            att = (acc[:, :LANES] * (1.0 / denom)).astype(BF16)
            att_tiles.extend(att[g * BLOCK:(g + 1) * BLOCK] for g in range(ATTN_GROUP))
        return jnp.concatenate(att_tiles, axis=1)

    for i0 in range(0, n_blocks, WO_BLOCKS):
        att = jnp.concatenate([block_attention(i) for i in range(i0, i0 + WO_BLOCKS)], axis=0)
        rows = pl.ds(i0 * BLOCK, WO_BLOCKS * BLOCK)
        o_ref[0, rows, :] = (x_ref[0, rows, :] + bo_ref[...]
                             + jnp.dot(att, wo_ref[...], preferred_element_type=F32))
    k_prev[1 - slot] = k_scr[:, tm:tm + BLOCK, :]
    v_prev[1 - slot] = v_scr[:, tm:tm + BLOCK, :]


def _alibi_bias():
    slopes = 2.0 ** (-8.0 * (np.arange(ATTN_HEADS) + 1) / ATTN_HEADS)
    q = np.arange(BLOCK)[:, None]
    j = np.arange(BLOCK)[None, :]
    cur = j <= q
    dist = np.where(cur, q - j, q - j + BLOCK).astype(np.float32)
    assert dist.min() >= 0 and dist.max() < WINDOW
    out = np.zeros((2, KV_PAIRS, 2, STACK, BLOCK), np.float32)
    for kp in range(KV_PAIRS):
        for f in range(2):
            for g in range(ATTN_GROUP):
                head = (2 * kp + f) * ATTN_GROUP + g
                base = -np.float32(slopes[head]) * dist
                out[0, kp, f, g * BLOCK:(g + 1) * BLOCK] = base
                out[1, kp, f, g * BLOCK:(g + 1) * BLOCK] = np.where(cur, base, -np.inf)
    return jnp.asarray(out)


def _to_kernel_head_order(a, axis):
    shape = a.shape
    split = shape[:axis] + (KV_PAIRS, 2, ATTN_GROUP, ATTN_HEAD_DIM) + shape[axis + 1:]
    return jnp.swapaxes(a.reshape(split), axis + 1, axis + 2).reshape(shape)


def _head_mean_matrix():
    idx = np.arange(MXU_WIDTH) // ATTN_HEAD_DIM
    return jnp.asarray((idx[:, None] == idx[None, :]).astype(np.float32) / ATTN_HEAD_DIM, BF16)


def _odd_mixer(x3, gain, wqkv, bqkv, qnorm, knorm, sinks, wo, bo):
    b, t, d = x3.shape
    tm = TM_ODD
    assert t % tm == 0 and tm % (WO_BLOCKS * BLOCK) == 0 and QK_W % MXU_WIDTH == 0
    qkv_w = wqkv.shape[1]
    gqk = jnp.concatenate([jnp.tile(qnorm, ATTN_HEADS) * (ATTN_HEAD_DIM ** -0.5),
                           jnp.tile(knorm, ATTN_KV_HEADS)]).reshape(1, QK_W)
    wqkv = jnp.concatenate([_to_kernel_head_order(wqkv[:, :Q_W], 1), wqkv[:, Q_W:]], axis=1)
    bqkv = jnp.concatenate([_to_kernel_head_order(bqkv[:Q_W], 0), bqkv[Q_W:]])
    wo = _to_kernel_head_order(wo, 0)
    n_forms = 2 * KV_PAIRS
    return pl.pallas_call(
        _odd_kernel,
        grid=(b, t // tm),
        in_specs=[
            pl.BlockSpec(memory_space=pltpu.SMEM),
            pl.BlockSpec((1, tm, d), lambda i, j: (i, j, 0)),
            _resident((1, d)),
            _resident((d, qkv_w)),
            _resident((1, qkv_w)),
            _resident((1, QK_W)),
            _resident((MXU_WIDTH, MXU_WIDTH)),
            _resident((2, KV_PAIRS, 2, STACK, BLOCK)),
            _resident((Q_W, d)),
            _resident((1, d)),
        ],
        out_specs=pl.BlockSpec((1, tm, d), lambda i, j: (i, j, 0)),
        out_shape=jax.ShapeDtypeStruct((b, t, d), F32),
        scratch_shapes=[
            pltpu.VMEM((tm, Q_W), BF16),
            pltpu.VMEM((n_forms, tm + BLOCK, LANES), BF16),
            pltpu.VMEM((n_forms, tm + BLOCK, 2 * LANES), BF16),
            pltpu.VMEM((2, n_forms, BLOCK, LANES), BF16),
            pltpu.VMEM((2, n_forms, BLOCK, 2 * LANES), BF16),
        ],
        compiler_params=_params(2),
        name="odd_mixer",
    )(sinks, x3, gain.reshape(1, d), wqkv, bqkv.reshape(1, qkv_w), gqk,
      _head_mean_matrix(), _alibi_bias(), wo, bo.reshape(1, d))


def kernel(x, mem, mem_norm, mem_wkv, mem_knorm, ffn1_norm, ffn1_wi, ffn1_wo, mix_norm, ssd_in_proj, ssd_conv_w, ssd_conv_b, ssd_dt_bias, ssd_a_log, ssd_d, ssd_norm, pool_w, pool_scale, even_out_proj, attn_wqkv, attn_bqkv, attn_qnorm, attn_knorm, attn_sinks, attn_wo, attn_bo, xattn_norm, xattn_wq, xattn_qnorm, xattn_wo, ffn2_norm, ffn2_wi, ffn2_wo):
    b, t, d = x.shape
    depth = ffn1_norm.shape[0]
    mem_kt, mem_v = _mem_kv(mem, mem_norm, mem_wkv, mem_knorm)
    pool_w2 = pool_w.reshape(pool_w.shape[0], D_POOL, POOL_GROUP)
    wi1, wo1 = _cast_weights([(ffn1_wi, 0), (ffn1_wo, 0)])
    for i in range(depth):
        jobs = [(ffn2_wi, i), (ffn2_wo, i), (xattn_wq, i), (xattn_wo, i)]
        if i % 2 == 0:
            jobs += [(pool_w2, i // 2), (even_out_proj, i // 2)]
        else:
            jobs += [(attn_wqkv, i // 2), (attn_wo, i // 2)]
        x2, (wi2, wo2, xq, xo, mix_a, mix_b) = _ffn(x.reshape(b * t, d), ffn1_norm[i], wi1, wo1, jobs)
        x = x2.reshape(b, t, d)
        if i % 2 == 0:
            e = i // 2
            x = _even_mixer(x, mix_norm[i], ssd_in_proj[e], ssd_conv_w[e], ssd_conv_b[e],
                            ssd_dt_bias[e], ssd_a_log[e], ssd_d[e], ssd_norm[e], mix_a,
                            pool_scale[e], mix_b)
        else:
            o = i // 2
            x = _odd_mixer(x, mix_norm[i], mix_a, attn_bqkv[o], attn_qnorm[o], attn_knorm[o],
                           attn_sinks[o], mix_b, attn_bo[o])
        x = _xattn(x, xattn_norm[i], xq, xattn_qnorm[i], mem_kt, mem_v, xo)
        jobs = [(ffn1_wi, i + 1), (ffn1_wo, i + 1)] if i + 1 < depth else []
        x2, nxt = _ffn(x.reshape(b * t, d), ffn2_norm[i], wi2, wo2, jobs)
        x = x2.reshape(b, t, d)
        if nxt:
            wi1, wo1 = nxt
    return x
```

```python
import numpy as np
import jax
import jax.numpy as jnp
from jax import lax
from jax.experimental import pallas as pl
from jax.experimental.pallas import tpu as pltpu

F32 = jnp.float32
BF16 = jnp.bfloat16

EPS = 1e-6
FFN_RESIDUAL = 0.5
SSD_HEADS = 16
SSD_HEAD_DIM = 64
SSD_GROUPS = 2
D_STATE = 128
CONV_WIDTH = 4
CHUNK = 128
D_SSM = SSD_HEADS * SSD_HEAD_DIM
GROUP_W = D_SSM // SSD_GROUPS
CONV_CH = D_SSM + 2 * SSD_GROUPS * D_STATE
POOL_WINDOWS = (2, 4, 8, 16)
POOL_GROUP = 256
D_POOL = POOL_GROUP * len(POOL_WINDOWS)
POOL_HALO = 16
CONV_HALO = 8
ATTN_HEADS = 16
ATTN_KV_HEADS = 4
ATTN_GROUP = ATTN_HEADS // ATTN_KV_HEADS
ATTN_HEAD_DIM = 64
WINDOW = 128
BLOCK = 128
MEM_HEADS = 4

LANES = 128
SUBLANES = 8
HALF = 64
MXU_WIDTH = 256
BF16_ROWS = 16

TM_FFN = 1024
TM_XATTN = 1024
TM_EVEN = 1024
TM_ODD = 1024
VMEM_LIMIT_BYTES = 56 * 1024 * 1024
FFN_CHUNKS = (1024, 1024, 768)


def _rms(x, gain):
    ms = jnp.mean(x * x, axis=-1, keepdims=True)
    return x * lax.rsqrt(ms + EPS) * gain


def _silu(x):
    return x * (1.0 / (1.0 + jnp.exp(-x)))


def _split2(x):
    hi = x.astype(BF16)
    lo = (x - hi.astype(F32)).astype(BF16)
    return hi, lo


def _resident(shape):
    nd = len(shape)
    return pl.BlockSpec(shape, lambda *_: (0,) * nd, pipeline_mode=pl.Buffered(1))


def _params(n_axes):
    return pltpu.CompilerParams(
        dimension_semantics=("arbitrary",) * n_axes,
        vmem_limit_bytes=VMEM_LIMIT_BYTES,
    )


def _memkv_kernel(mem_ref, g_ref, wkv_ref, kn_ref, *refs):
    n_jobs = (len(refs) - 2) // 2
    kt_ref, v_ref = refs[n_jobs], refs[n_jobs + 1]
    _run_cast_jobs(refs[:n_jobs], refs[n_jobs + 2:])
    d = mem_ref.shape[-1]
    hd = d // MEM_HEADS
    h = _rms(mem_ref[0], g_ref[...]).astype(BF16)
    kv = jnp.dot(h, wkv_ref[...].astype(BF16), preferred_element_type=F32)
    for i in range(MEM_HEADS):
        kh = _rms(kv[:, i * hd:(i + 1) * hd], kn_ref[...])
        kt_ref[0, i] = kh.T.astype(BF16)
    v_ref[0] = kv[:, d:].astype(BF16)


def _mem_kv(mem, mem_norm, wkv, mem_knorm, cast_jobs=()):
    b, m, d = mem.shape
    hd = d // MEM_HEADS
    job_in, job_out, job_shapes = _cast_job_specs(cast_jobs, b)
    outs = pl.pallas_call(
        _memkv_kernel,
        grid=(b,),
        in_specs=[
            pl.BlockSpec((1, m, d), lambda i: (i, 0, 0)),
            _resident((1, d)),
            _resident((d, 2 * d)),
            _resident((1, hd)),
        ] + job_in,
        out_specs=[
            pl.BlockSpec((1, MEM_HEADS, hd, m), lambda i: (i, 0, 0, 0)),
            pl.BlockSpec((1, m, d), lambda i: (i, 0, 0)),
        ] + job_out,
        out_shape=[
            jax.ShapeDtypeStruct((b, MEM_HEADS, hd, m), BF16),
            jax.ShapeDtypeStruct((b, m, d), BF16),
        ] + job_shapes,
        compiler_params=_params(1),
        name="mem_kv",
    )(mem, mem_norm.reshape(1, d), wkv, mem_knorm.reshape(1, hd), *[src for src, _ in cast_jobs])
    return outs[0], outs[1], outs[2:]


def _cast_job_specs(jobs, n_steps):
    in_specs, out_specs, out_shapes = [], [], []
    for src, layer in jobs:
        rows, cols = src.shape[-2:]
        rb = next(r for r in range(BF16_ROWS, rows + 1, BF16_ROWS)
                  if rows % r == 0 and rows // r <= n_steps)
        last = rows // rb - 1
        if layer is None:
            in_specs.append(pl.BlockSpec((rb, cols), lambda i, last=last: (jnp.minimum(i, last), 0)))
        else:
            in_specs.append(pl.BlockSpec(
                (None, rb, cols), lambda i, last=last, layer=layer: (layer, jnp.minimum(i, last), 0)))
        out_specs.append(pl.BlockSpec((rb, cols), lambda i, last=last: (jnp.minimum(i, last), 0)))
        out_shapes.append(jax.ShapeDtypeStruct((rows, cols), BF16))
    return in_specs, out_specs, out_shapes


def _run_cast_jobs(in_refs, out_refs):
    for src_ref, dst_ref in zip(in_refs, out_refs):
        dst_ref[...] = src_ref[...].astype(BF16)


def _ffn_kernel(x_ref, g_ref, wi_ref, wo_ref, *refs):
    n_jobs = (len(refs) - 2) // 2
    o_ref, act_ref = refs[n_jobs], refs[-1]
    _run_cast_jobs(refs[:n_jobs], refs[n_jobs + 1:2 * n_jobs + 1])
    d_ff = wo_ref.shape[0]
    h = _rms(x_ref[...], g_ref[...]).astype(BF16)
    lo = 0
    for ck in FFN_CHUNKS:
        gate = jnp.dot(h, wi_ref[:, lo:lo + ck], preferred_element_type=F32)
        up = jnp.dot(h, wi_ref[:, d_ff + lo:d_ff + lo + ck], preferred_element_type=F32)
        act_ref[:, lo:lo + ck] = (_silu(gate) * up).astype(BF16)
        lo += ck
    y = jnp.dot(act_ref[...], wo_ref[...], preferred_element_type=F32)
    o_ref[...] = x_ref[...] + FFN_RESIDUAL * y


def _ffn(x2, gain, wi, wo, cast_jobs=()):
    n, d = x2.shape
    d_ff = wo.shape[0]
    assert sum(FFN_CHUNKS) == d_ff and n % TM_FFN == 0
    n_steps = n // TM_FFN
    job_in, job_out, job_shapes = _cast_job_specs(cast_jobs, n_steps)
    outs = pl.pallas_call(
        _ffn_kernel,
        grid=(n_steps,),
        in_specs=[
            pl.BlockSpec((TM_FFN, d), lambda i: (i, 0)),
            _resident((1, d)),
            _resident((d, 2 * d_ff)),
            _resident((d_ff, d)),
        ] + job_in,
        out_specs=[pl.BlockSpec((TM_FFN, d), lambda i: (i, 0))] + job_out,
        out_shape=[jax.ShapeDtypeStruct((n, d), F32)] + job_shapes,
        scratch_shapes=[pltpu.VMEM((TM_FFN, d_ff), BF16)],
        compiler_params=_params(1),
        name="ffn",
    )(x2, gain.reshape(1, d), wi, wo, *[src for src, _ in cast_jobs])
    return outs[0], outs[1:]


def _xattn_kernel(x_ref, g_ref, wq_ref, qn_ref, kt_ref, v_ref, wo_ref, o_ref, att_ref):
    d = x_ref.shape[-1]
    hd = d // MEM_HEADS
    h = _rms(x_ref[0], g_ref[...]).astype(BF16)
    q = jnp.dot(h, wq_ref[...], preferred_element_type=F32)
    for i in range(MEM_HEADS):
        cols = slice(i * hd, (i + 1) * hd)
        qh = (_rms(q[:, cols], qn_ref[...]) * (hd ** -0.5)).astype(BF16)
        s = jnp.dot(qh, kt_ref[0, i], preferred_element_type=F32)
        p = jnp.exp(s - jnp.max(s, axis=-1, keepdims=True))
        inv = 1.0 / jnp.sum(p, axis=-1, keepdims=True)
        o = jnp.dot(p.astype(BF16), v_ref[0, :, cols], preferred_element_type=F32)
        att_ref[:, cols] = (o * inv).astype(BF16)
    y = jnp.dot(att_ref[...], wo_ref[...], preferred_element_type=F32)
    o_ref[0] = x_ref[0] + y


def _xattn(x3, gain, wq, qnorm, mem_kt, mem_v, wo):
    b, t, d = x3.shape
    hd = d // MEM_HEADS
    m = mem_v.shape[1]
    tm = TM_XATTN
    assert t % tm == 0
    return pl.pallas_call(
        _xattn_kernel,
        grid=(b, t // tm),
        in_specs=[
            pl.BlockSpec((1, tm, d), lambda i, j: (i, j, 0)),
            _resident((1, d)),
            _resident((d, d)),
            _resident((1, hd)),
            pl.BlockSpec((1, MEM_HEADS, hd, m), lambda i, j: (i, 0, 0, 0)),
            pl.BlockSpec((1, m, d), lambda i, j: (i, 0, 0)),
            _resident((d, d)),
        ],
        out_specs=pl.BlockSpec((1, tm, d), lambda i, j: (i, j, 0)),
        out_shape=jax.ShapeDtypeStruct((b, t, d), F32),
        scratch_shapes=[pltpu.VMEM((tm, d), BF16)],
        compiler_params=_params(2),
        name="xattn",
    )(x3, gain.reshape(1, d), wq, qnorm.reshape(1, hd), mem_kt, mem_v, wo)


CONV_TILES = CONV_CH // LANES
POOL_TILES = D_POOL // LANES
TILES_PER_POOL_GROUP = POOL_GROUP // LANES


def _even_kernel(x_ref, g_ref, wzx_ref, wpool_ref, wdt_ref, cw_ref, cb_ref, dtb_ref, alog_ref,
                 dskip_ref, norm_ref,
                 pw_ref, ps_ref, wout_ref, exph_ref, tril_ref, o_ref,
                 conv_scr, act_scr, dt_scr, z_scr, pool_scr, state_scr, conv_halo, pool_halo):
    t = pl.program_id(1)
    tm = x_ref.shape[1]
    n_chunks = tm // CHUNK

    slot = lax.rem(t, 2)

    @pl.when(t == 0)
    def _():
        conv_halo[0] = jnp.zeros((CONV_TILES, CONV_HALO, LANES), F32)
        pool_halo[0] = jnp.zeros((POOL_TILES, POOL_HALO, LANES), F32)
        state_scr[...] = jnp.zeros(state_scr.shape, F32)

    conv_scr[:, 0:CONV_HALO, :] = conv_halo[slot]
    pool_scr[:, 0:POOL_HALO, :] = pool_halo[slot]

    x = x_ref[0]
    h = _rms(x, g_ref[...]).astype(BF16)
    xbc = jnp.dot(h, wzx_ref[:, D_SSM:], preferred_element_type=F32)
    for c in range(CONV_TILES):
        conv_scr[c, CONV_HALO:CONV_HALO + tm, :] = xbc[:, c * LANES:(c + 1) * LANES]
    u_pool = jnp.dot(h, wpool_ref[...], preferred_element_type=F32)
    for c in range(POOL_TILES):
        pool_scr[c, POOL_HALO:POOL_HALO + tm, :] = u_pool[:, c * LANES:(c + 1) * LANES]
    dt_raw = jnp.dot(h, wdt_ref[...], preferred_element_type=F32) + dtb_ref[...]
    dt_scr[...] = jnp.maximum(dt_raw, 0.0) + jnp.log1p(jnp.exp(-jnp.abs(dt_raw)))
    z_scr[...] = jnp.dot(h, wzx_ref[:, 0:D_SSM], preferred_element_type=F32)

    for c in range(CONV_TILES):
        cols = slice(c * LANES, (c + 1) * LANES)
        acc = cb_ref[:, cols]
        for k in range(CONV_WIDTH):
            acc = acc + (conv_scr[c, pl.ds(CONV_HALO - CONV_WIDTH + 1 + k, tm), :]
                         * cw_ref[k:k + 1, cols])
        act_scr[:, cols] = _silu(acc)
    conv_halo[1 - slot] = conv_scr[:, tm:tm + CONV_HALO, :]

    pos = t * tm + lax.broadcasted_iota(jnp.int32, (tm, 1), 0)
    y_pool = []
    for k, w in enumerate(POOL_WINDOWS):
        inv_count = 1.0 / jnp.minimum(pos + 1, w).astype(F32)
        pooled = []
        for c in range(k * TILES_PER_POOL_GROUP, (k + 1) * TILES_PER_POOL_GROUP):
            if w > SUBLANES:
                e = pool_scr[c, pl.ds(POOL_HALO - SUBLANES, tm + SUBLANES), :]
                for i in range(1, SUBLANES):
                    e = e + pool_scr[c, pl.ds(POOL_HALO - SUBLANES - i, tm + SUBLANES), :]
                s = e[SUBLANES:, :] + e[:tm, :]
            else:
                s = pool_scr[c, pl.ds(POOL_HALO, tm), :]
                for i in range(1, w):
                    s = s + pool_scr[c, pl.ds(POOL_HALO - i, tm), :]
            pooled.append(s * inv_count - pool_scr[c, pl.ds(POOL_HALO, tm), :])
        pooled = jnp.concatenate(pooled, axis=1).astype(BF16)
        kcols = slice(k * POOL_GROUP, (k + 1) * POOL_GROUP)
        yk = jnp.dot(pooled, pw_ref[k], preferred_element_type=F32)
        y_pool.append((yk * ps_ref[:, kcols]).astype(BF16))
    pool_halo[1 - slot] = pool_scr[:, tm:tm + POOL_HALO, :]

    a_row = -jnp.exp(alog_ref[...])
    low_half = (lax.broadcasted_iota(jnp.int32, (CHUNK, D_SSM), 1) & HALF) == 0
    causal = (lax.broadcasted_iota(jnp.int32, (CHUNK, CHUNK), 0)
              >= lax.broadcasted_iota(jnp.int32, (CHUNK, CHUNK), 1))
    heads_per_group = SSD_HEADS // SSD_GROUPS

    def chunk_body(c):
        rows = pl.ds(c * CHUNK, CHUNK)
        xs = act_scr[rows, 0:D_SSM]
        dt = dt_scr[rows, :]
        adt = dt * a_row
        hi = adt.astype(BF16)
        r1 = adt - hi.astype(F32)
        mid = r1.astype(BF16)
        lo = (r1 - mid.astype(F32)).astype(BF16)
        acs = jnp.dot(tril_ref[...], jnp.concatenate([hi, mid, lo], axis=0),
                      preferred_element_type=F32)
        acs_last = acs[CHUNK - 1:CHUNK, :]
        stacked = jnp.concatenate([dt, jnp.exp(acs_last - acs), jnp.exp(acs)], axis=0)
        s_hi, s_lo = _split2(stacked)
        expanded = jnp.dot(jnp.concatenate([s_hi, s_lo], axis=1), exph_ref[...],
                           preferred_element_type=F32)
        dt_exp = expanded[0:CHUNK]
        dte_exp = expanded[CHUNK:2 * CHUNK]
        dfs_exp = expanded[2 * CHUNK:3 * CHUNK]
        xdt = xs * dt_exp
        xw_b = (xdt * dte_exp).astype(BF16)
        x_even = jnp.where(low_half, xdt, 0.0).astype(BF16)
        x_odd = jnp.where(low_half, 0.0, xdt).astype(BF16)
        acs_row = acs.T

        y_tiles = []
        for g in range(SSD_GROUPS):
            gcols = slice(g * GROUP_W, (g + 1) * GROUP_W)
            b_g = act_scr[rows, D_SSM + g * D_STATE:D_SSM + (g + 1) * D_STATE]
            c_g = act_scr[rows, D_SSM + SSD_GROUPS * D_STATE + g * D_STATE:
                          D_SSM + SSD_GROUPS * D_STATE + (g + 1) * D_STATE].astype(BF16)
            bt_g = b_g.T.astype(BF16)
            cb = jnp.dot(c_g, bt_g, preferred_element_type=F32)
            s_new = jnp.dot(bt_g, xw_b[:, gcols], preferred_element_type=F32)
            s_in = state_scr[g]
            y_off = jnp.dot(c_g, s_in.astype(BF16), preferred_element_type=F32) * dfs_exp[:, gcols]
            state_scr[g] = s_in * dfs_exp[CHUNK - 1:CHUNK, gcols] + s_new
            for pair in range(heads_per_group // 2):
                pcols = slice(g * GROUP_W + pair * LANES, g * GROUP_W + (pair + 1) * LANES)
                w2 = []
                for parity in range(2):
                    hh = g * heads_per_group + 2 * pair + parity
                    seg = jnp.where(causal, jnp.exp(acs[:, hh:hh + 1] - acs_row[hh:hh + 1, :]), 0.0)
                    w2.append((cb * seg).astype(BF16))
                y_diag = jnp.dot(jnp.concatenate(w2, axis=1),
                                 jnp.concatenate([x_even[:, pcols], x_odd[:, pcols]], axis=0),
                                 preferred_element_type=F32)
                y_tiles.append(y_diag + y_off[:, pair * LANES:(pair + 1) * LANES])
        y = (jnp.concatenate(y_tiles, axis=1) + dskip_ref[...] * xs) * _silu(z_scr[rows, :])
        return jnp.concatenate(
            [_rms(y[:, g * GROUP_W:(g + 1) * GROUP_W], norm_ref[:, g * GROUP_W:(g + 1) * GROUP_W])
             for g in range(SSD_GROUPS)], axis=1).astype(BF16)

    y_ssd = jnp.concatenate([chunk_body(c) for c in range(n_chunks)], axis=0)
    o_ref[0] = x + jnp.dot(jnp.concatenate([y_ssd] + y_pool, axis=1), wout_ref[...],
                           preferred_element_type=F32)


def _head_expand_matrix():
    m = np.zeros((2 * LANES, D_SSM), np.float32)
    for hh in range(SSD_HEADS):
        m[hh, hh * SSD_HEAD_DIM:(hh + 1) * SSD_HEAD_DIM] = 1.0
        m[LANES + hh, hh * SSD_HEAD_DIM:(hh + 1) * SSD_HEAD_DIM] = 1.0
    return jnp.asarray(m, BF16)


def _even_mixer(x3, gain, in_proj, conv_w, conv_b, dt_bias, a_log, d_skip, ssd_norm,
                pool_w, pool_scale, out_proj):
    b, t, d = x3.shape
    tm = TM_EVEN
    assert t % tm == 0 and tm % CHUNK == 0
    o_dt = D_SSM + CONV_CH
    o_pool = o_dt + SSD_HEADS
    pad = LANES - SSD_HEADS
    w_bf = in_proj.astype(BF16)
    w_zx, w_pool = w_bf[:, :o_dt], w_bf[:, o_pool:]
    w_dt = jnp.pad(w_bf[:, o_dt:o_pool], ((0, 0), (0, pad)))
    dtb = jnp.pad(dt_bias, (0, pad)).reshape(1, LANES)
    alog = jnp.pad(a_log, (0, pad)).reshape(1, LANES)
    dskip = jnp.repeat(d_skip, SSD_HEAD_DIM).reshape(1, D_SSM)
    tril3 = jnp.asarray(np.tile(np.tril(np.ones((CHUNK, CHUNK), np.float32)), (1, 3)), BF16)
    n_g = len(POOL_WINDOWS)
    return pl.pallas_call(
        _even_kernel,
        grid=(b, t // tm),
        in_specs=[
            pl.BlockSpec((1, tm, d), lambda i, j: (i, j, 0)),
            _resident((1, d)),
            _resident((d, o_dt)),
            _resident((d, D_POOL)),
            _resident((d, LANES)),
            _resident((CONV_WIDTH, CONV_CH)),
            _resident((1, CONV_CH)),
            _resident((1, LANES)),
            _resident((1, LANES)),
            _resident((1, D_SSM)),
            _resident((1, D_SSM)),
            _resident((n_g, POOL_GROUP, POOL_GROUP)),
            _resident((1, D_POOL)),
            _resident((D_SSM + D_POOL, d)),
            _resident((2 * LANES, D_SSM)),
            _resident((CHUNK, 3 * CHUNK)),
        ],
        out_specs=pl.BlockSpec((1, tm, d), lambda i, j: (i, j, 0)),
        out_shape=jax.ShapeDtypeStruct((b, t, d), F32),
        scratch_shapes=[
            pltpu.VMEM((CONV_TILES, tm + CONV_HALO, LANES), F32),
            pltpu.VMEM((tm, CONV_CH), F32),
            pltpu.VMEM((tm, LANES), F32),
            pltpu.VMEM((tm, D_SSM), F32),
            pltpu.VMEM((POOL_TILES, tm + POOL_HALO, LANES), F32),
            pltpu.VMEM((SSD_GROUPS, D_STATE, GROUP_W), F32),
            pltpu.VMEM((2, CONV_TILES, CONV_HALO, LANES), F32),
            pltpu.VMEM((2, POOL_TILES, POOL_HALO, LANES), F32),
        ],
        compiler_params=_params(2),
        name="even_mixer",
    )(x3, gain.reshape(1, d), w_zx, w_pool, w_dt, conv_w, conv_b.reshape(1, CONV_CH), dtb, alog,
      dskip,
      ssd_norm.reshape(1, D_SSM), pool_w.reshape(n_g, POOL_GROUP, POOL_GROUP),
      pool_scale.reshape(1, D_POOL), out_proj, _head_expand_matrix(), tril3)


N_QK_HEADS = ATTN_HEADS + ATTN_KV_HEADS
QK_W = N_QK_HEADS * ATTN_HEAD_DIM
Q_W = ATTN_HEADS * ATTN_HEAD_DIM
KV_W = ATTN_KV_HEADS * ATTN_HEAD_DIM
KV_PAIRS = ATTN_KV_HEADS // 2
STACK = ATTN_GROUP * BLOCK
WO_BLOCKS = 4


def _odd_kernel(sink_ref, x_ref, g_ref, wqkv_ref, bqkv_ref, gqk_ref, headmean_ref, bias_ref,
                wo_ref, bo_ref, o_ref, q_scr, k_scr, v_scr, k_prev, v_prev):
    t = pl.program_id(1)
    tm = x_ref.shape[1]
    n_blocks = tm // BLOCK
    n_forms = 2 * KV_PAIRS

    slot = lax.rem(t, 2)

    @pl.when(t == 0)
    def _():
        k_prev[0] = jnp.zeros((n_forms, BLOCK, LANES), BF16)
        v_prev[0] = jnp.zeros((n_forms, BLOCK, 2 * LANES), BF16)

    k_scr[:, 0:BLOCK, :] = k_prev[slot]
    v_scr[:, 0:BLOCK, :] = v_prev[slot]

    x = x_ref[0]
    h = _rms(x, g_ref[...]).astype(BF16)
    qkv = jnp.dot(h, wqkv_ref[...], preferred_element_type=F32) + bqkv_ref[...]
    qk = qkv[:, 0:QK_W]
    sq = (qk * qk).astype(BF16)
    msq = jnp.concatenate(
        [jnp.dot(sq[:, c0:c0 + MXU_WIDTH], headmean_ref[...], preferred_element_type=F32)
         for c0 in range(0, QK_W, MXU_WIDTH)], axis=1)
    qkn = qk * lax.rsqrt(msq + EPS) * gqk_ref[...]
    q_scr[...] = qkn[:, 0:Q_W].astype(BF16)

    low_half = lax.broadcasted_iota(jnp.int32, (tm, LANES), 1) < HALF
    zeros = jnp.zeros((tm, LANES), F32)
    ones = (jnp.where(low_half, 1.0, 0.0).astype(BF16), jnp.where(low_half, 0.0, 1.0).astype(BF16))
    for is_v, src, dst, base in ((False, qkn, k_scr, Q_W), (True, qkv, v_scr, QK_W)):
        for kp in range(KV_PAIRS):
            a2 = src[:, base + kp * LANES:base + (kp + 1) * LANES]
            for f, val in enumerate((jnp.where(low_half, a2, zeros), jnp.where(low_half, zeros, a2))):
                dst[2 * kp + f, BLOCK:BLOCK + tm, 0:LANES] = val.astype(BF16)
                if is_v:
                    dst[2 * kp + f, BLOCK:BLOCK + tm, LANES:2 * LANES] = ones[f]

    nt_dims = (((1,), (1,)), ((), ()))
    row = lax.broadcasted_iota(jnp.int32, (STACK, LANES), 0)
    col = lax.broadcasted_iota(jnp.int32, (STACK, LANES), 1)
    tri = col <= (row & (BLOCK - 1))
    lane_low = col < HALF
    row_group = lax.broadcasted_iota(jnp.int32, (STACK, 1), 0) // BLOCK

    def sink_column(kvh):
        sink = jnp.full((STACK, 1), sink_ref[kvh * ATTN_GROUP], F32)
        for g in range(1, ATTN_GROUP):
            sink = jnp.where(row_group == g, sink_ref[kvh * ATTN_GROUP + g], sink)
        return sink

    def block_attention(i):
        rows = pl.ds(i * BLOCK, BLOCK)
        krows = pl.ds(i * BLOCK, 2 * BLOCK)
        sel = jnp.where(t == 0, 1, 0) if i == 0 else 0
        att_tiles = []
        for kp in range(KV_PAIRS):
            col0 = kp * ATTN_GROUP * LANES
            q4 = jnp.concatenate(
                [q_scr[rows, col0 + g * LANES:col0 + (g + 1) * LANES] for g in range(ATTN_GROUP)],
                axis=0)
            acc = None
            shifts = []
            for f in range(2):
                s2 = lax.dot_general(q4, k_scr[2 * kp + f, krows, :], nt_dims,
                                     preferred_element_type=F32)
                sc = jnp.where(tri, s2[:, BLOCK:], s2[:, :BLOCK]) + bias_ref[sel, kp, f]
                sink = sink_column(2 * kp + f)
                m = jnp.maximum(jnp.max(sc, axis=-1, keepdims=True), sink)
                p = jnp.exp(sc - m)
                pcat = jnp.concatenate([jnp.where(tri, 0.0, p), jnp.where(tri, p, 0.0)],
                                       axis=1).astype(BF16)
                part = jnp.dot(pcat, v_scr[2 * kp + f, krows, :],
                               preferred_element_type=F32)
                acc = part if acc is None else acc + part
                shifts.append(sink - m)
            denom = acc[:, LANES:] + jnp.exp(jnp.where(lane_low, shifts[0], shifts[1]))
            att = (acc[:, :LANES] * (1.0 / denom)).astype(BF16)
            att_tiles.extend(att[g * BLOCK:(g + 1) * BLOCK] for g in range(ATTN_GROUP))
        return jnp.concatenate(att_tiles, axis=1)

    for i0 in range(0, n_blocks, WO_BLOCKS):
        att = jnp.concatenate([block_attention(i) for i in range(i0, i0 + WO_BLOCKS)], axis=0)
        rows = pl.ds(i0 * BLOCK, WO_BLOCKS * BLOCK)
        o_ref[0, rows, :] = (x_ref[0, rows, :] + bo_ref[...]
                             + jnp.dot(att, wo_ref[...], preferred_element_type=F32))
    k_prev[1 - slot] = k_scr[:, tm:tm + BLOCK, :]
    v_prev[1 - slot] = v_scr[:, tm:tm + BLOCK, :]


def _alibi_bias():
    slopes = 2.0 ** (-8.0 * (np.arange(ATTN_HEADS) + 1) / ATTN_HEADS)
    q = np.arange(BLOCK)[:, None]
    j = np.arange(BLOCK)[None, :]
    cur = j <= q
    dist = np.where(cur, q - j, q - j + BLOCK).astype(np.float32)
    assert dist.min() >= 0 and dist.max() < WINDOW
    out = np.zeros((2, KV_PAIRS, 2, STACK, BLOCK), np.float32)
    for kp in range(KV_PAIRS):
        for f in range(2):
            for g in range(ATTN_GROUP):
                head = (2 * kp + f) * ATTN_GROUP + g
                base = -np.float32(slopes[head]) * dist
                out[0, kp, f, g * BLOCK:(g + 1) * BLOCK] = base
                out[1, kp, f, g * BLOCK:(g + 1) * BLOCK] = np.where(cur, base, -np.inf)
    return jnp.asarray(out)


def _to_kernel_head_order(a, axis):
    shape = a.shape
    split = shape[:axis] + (KV_PAIRS, 2, ATTN_GROUP, ATTN_HEAD_DIM) + shape[axis + 1:]
    return jnp.swapaxes(a.reshape(split), axis + 1, axis + 2).reshape(shape)


def _head_mean_matrix():
    idx = np.arange(MXU_WIDTH) // ATTN_HEAD_DIM
    return jnp.asarray((idx[:, None] == idx[None, :]).astype(np.float32) / ATTN_HEAD_DIM, BF16)


def _odd_mixer(x3, gain, wqkv, bqkv, qnorm, knorm, sinks, wo, bo):
    b, t, d = x3.shape
    tm = TM_ODD
    assert t % tm == 0 and tm % (WO_BLOCKS * BLOCK) == 0 and QK_W % MXU_WIDTH == 0
    qkv_w = wqkv.shape[1]
    gqk = jnp.concatenate([jnp.tile(qnorm, ATTN_HEADS) * (ATTN_HEAD_DIM ** -0.5),
                           jnp.tile(knorm, ATTN_KV_HEADS)]).reshape(1, QK_W)
    wqkv = jnp.concatenate([_to_kernel_head_order(wqkv[:, :Q_W], 1), wqkv[:, Q_W:]], axis=1)
    bqkv = jnp.concatenate([_to_kernel_head_order(bqkv[:Q_W], 0), bqkv[Q_W:]])
    wo = _to_kernel_head_order(wo, 0)
    n_forms = 2 * KV_PAIRS
    return pl.pallas_call(
        _odd_kernel,
        grid=(b, t // tm),
        in_specs=[
            pl.BlockSpec(memory_space=pltpu.SMEM),
            pl.BlockSpec((1, tm, d), lambda i, j: (i, j, 0)),
            _resident((1, d)),
            _resident((d, qkv_w)),
            _resident((1, qkv_w)),
            _resident((1, QK_W)),
            _resident((MXU_WIDTH, MXU_WIDTH)),
            _resident((2, KV_PAIRS, 2, STACK, BLOCK)),
            _resident((Q_W, d)),
            _resident((1, d)),
        ],
        out_specs=pl.BlockSpec((1, tm, d), lambda i, j: (i, j, 0)),
        out_shape=jax.ShapeDtypeStruct((b, t, d), F32),
        scratch_shapes=[
            pltpu.VMEM((tm, Q_W), BF16),
            pltpu.VMEM((n_forms, tm + BLOCK, LANES), BF16),
            pltpu.VMEM((n_forms, tm + BLOCK, 2 * LANES), BF16),
            pltpu.VMEM((2, n_forms, BLOCK, LANES), BF16),
            pltpu.VMEM((2, n_forms, BLOCK, 2 * LANES), BF16),
        ],
        compiler_params=_params(2),
        name="odd_mixer",
    )(sinks, x3, gain.reshape(1, d), wqkv, bqkv.reshape(1, qkv_w), gqk,
      _head_mean_matrix(), _alibi_bias(), wo, bo.reshape(1, d))


def kernel(x, mem, mem_norm, mem_wkv, mem_knorm, ffn1_norm, ffn1_wi, ffn1_wo, mix_norm, ssd_in_proj, ssd_conv_w, ssd_conv_b, ssd_dt_bias, ssd_a_log, ssd_d, ssd_norm, pool_w, pool_scale, even_out_proj, attn_wqkv, attn_bqkv, attn_qnorm, attn_knorm, attn_sinks, attn_wo, attn_bo, xattn_norm, xattn_wq, xattn_qnorm, xattn_wo, ffn2_norm, ffn2_wi, ffn2_wo):
    b, t, d = x.shape
    depth = ffn1_norm.shape[0]
    mem_kt, mem_v, (wi1, wo1) = _mem_kv(mem, mem_norm, mem_wkv, mem_knorm,
                                        [(ffn1_wi, 0), (ffn1_wo, 0)])
    pool_w2 = pool_w.reshape(pool_w.shape[0], D_POOL, POOL_GROUP)
    for i in range(depth):
        jobs = [(ffn2_wi, i), (ffn2_wo, i), (xattn_wq, i), (xattn_wo, i)]
        if i % 2 == 0:
            jobs += [(pool_w2, i // 2), (even_out_proj, i // 2)]
        else:
            jobs += [(attn_wqkv, i // 2), (attn_wo, i // 2)]
        x2, (wi2, wo2, xq, xo, mix_a, mix_b) = _ffn(x.reshape(b * t, d), ffn1_norm[i], wi1, wo1, jobs)
        x = x2.reshape(b, t, d)
        if i % 2 == 0:
            e = i // 2
            x = _even_mixer(x, mix_norm[i], ssd_in_proj[e], ssd_conv_w[e], ssd_conv_b[e],
                            ssd_dt_bias[e], ssd_a_log[e], ssd_d[e], ssd_norm[e], mix_a,
                            pool_scale[e], mix_b)
        else:
            o = i // 2
            x = _odd_mixer(x, mix_norm[i], mix_a, attn_bqkv[o], attn_qnorm[o], attn_knorm[o],
                           attn_sinks[o], mix_b, attn_bo[o])
        x = _xattn(x, xattn_norm[i], xq, xattn_qnorm[i], mem_kt, mem_v, xo)
        jobs = [(ffn1_wi, i + 1), (ffn1_wo, i + 1)] if i + 1 < depth else []
        x2, nxt = _ffn(x.reshape(b * t, d), ffn2_norm[i], wi2, wo2, jobs)
        x = x2.reshape(b, t, d)
        if nxt:
            wi1, wo1 = nxt
    return x
```

```python
import numpy as np
import jax
import jax.numpy as jnp
from jax import lax
from jax.experimental import pallas as pl
from jax.experimental.pallas import tpu as pltpu

F32 = jnp.float32
BF16 = jnp.bfloat16

EPS = 1e-6
LOG2_E = 1.4426950408889634
FFN_RESIDUAL = 0.5
SSD_HEADS = 16
SSD_HEAD_DIM = 64
SSD_GROUPS = 2
D_STATE = 128
CONV_WIDTH = 4
CHUNK = 128
D_SSM = SSD_HEADS * SSD_HEAD_DIM
GROUP_W = D_SSM // SSD_GROUPS
CONV_CH = D_SSM + 2 * SSD_GROUPS * D_STATE
POOL_WINDOWS = (2, 4, 8, 16)
POOL_GROUP = 256
D_POOL = POOL_GROUP * len(POOL_WINDOWS)
POOL_HALO = 16
CONV_HALO = 8
ATTN_HEADS = 16
ATTN_KV_HEADS = 4
ATTN_GROUP = ATTN_HEADS // ATTN_KV_HEADS
ATTN_HEAD_DIM = 64
WINDOW = 128
BLOCK = 128
MEM_HEADS = 4

LANES = 128
SUBLANES = 8
HALF = 64
MXU_WIDTH = 256
BF16_ROWS = 16
CAST_STEPS = 8

TM_FFN = 1024
TM_XATTN = 1024
TM_EVEN = 1024
TM_ODD = 1024
VMEM_LIMIT_BYTES = 56 * 1024 * 1024
FFN_CHUNKS = (1024, 1024, 768)


def _rms(x, gain):
    ms = jnp.mean(x * x, axis=-1, keepdims=True)
    return x * lax.rsqrt(ms + EPS) * gain


def _silu(x):
    return x * (1.0 / (1.0 + jnp.exp(-x)))


def _split2(x):
    hi = x.astype(BF16)
    lo = (x - hi.astype(F32)).astype(BF16)
    return hi, lo


def _resident(shape):
    nd = len(shape)
    return pl.BlockSpec(shape, lambda *_: (0,) * nd, pipeline_mode=pl.Buffered(1))


def _params(n_axes):
    return pltpu.CompilerParams(
        dimension_semantics=("arbitrary",) * n_axes,
        vmem_limit_bytes=VMEM_LIMIT_BYTES,
    )


def _memkv_kernel(mem_ref, g_ref, wkv_ref, kn_ref, kt_ref, v_ref):
    d = mem_ref.shape[-1]
    hd = d // MEM_HEADS
    h = _rms(mem_ref[0], g_ref[...]).astype(BF16)
    kv = jnp.dot(h, wkv_ref[...], preferred_element_type=F32)
    for i in range(MEM_HEADS):
        kh = _rms(kv[:, i * hd:(i + 1) * hd], kn_ref[...])
        kt_ref[0, i] = kh.T.astype(BF16)
    v_ref[0] = kv[:, d:].astype(BF16)


def _mem_kv(mem, mem_norm, wkv, mem_knorm):
    b, m, d = mem.shape
    hd = d // MEM_HEADS
    return pl.pallas_call(
        _memkv_kernel,
        grid=(b,),
        in_specs=[
            pl.BlockSpec((1, m, d), lambda i: (i, 0, 0)),
            _resident((1, d)),
            _resident((d, 2 * d)),
            _resident((1, hd)),
        ],
        out_specs=[
            pl.BlockSpec((1, MEM_HEADS, hd, m), lambda i: (i, 0, 0, 0)),
            pl.BlockSpec((1, m, d), lambda i: (i, 0, 0)),
        ],
        out_shape=[
            jax.ShapeDtypeStruct((b, MEM_HEADS, hd, m), BF16),
            jax.ShapeDtypeStruct((b, m, d), BF16),
        ],
        compiler_params=_params(1),
        name="mem_kv",
    )(mem, mem_norm.reshape(1, d), wkv.astype(BF16), mem_knorm.reshape(1, hd))


def _cast_job_specs(jobs, n_steps):
    in_specs, out_specs, out_shapes = [], [], []
    for src, layer in jobs:
        rows, cols = src.shape[-2:]
        rb = next(r for r in range(BF16_ROWS, rows + 1, BF16_ROWS)
                  if rows % r == 0 and rows // r <= n_steps)
        last = rows // rb - 1
        if layer is None:
            in_specs.append(pl.BlockSpec((rb, cols), lambda i, last=last: (jnp.minimum(i, last), 0)))
        else:
            in_specs.append(pl.BlockSpec(
                (None, rb, cols), lambda i, last=last, layer=layer: (layer, jnp.minimum(i, last), 0)))
        out_specs.append(pl.BlockSpec((rb, cols), lambda i, last=last: (jnp.minimum(i, last), 0)))
        out_shapes.append(jax.ShapeDtypeStruct((rows, cols), BF16))
    return in_specs, out_specs, out_shapes


def _run_cast_jobs(in_refs, out_refs):
    for src_ref, dst_ref in zip(in_refs, out_refs):
        dst_ref[...] = src_ref[...].astype(BF16)


def _cast_kernel(*refs):
    n = len(refs) // 2
    _run_cast_jobs(refs[:n], refs[n:])


def _cast_weights(jobs):
    in_specs, out_specs, out_shapes = _cast_job_specs(jobs, CAST_STEPS)
    return pl.pallas_call(
        _cast_kernel,
        grid=(CAST_STEPS,),
        in_specs=in_specs,
        out_specs=out_specs,
        out_shape=out_shapes,
        compiler_params=_params(1),
        name="cast_weights",
    )(*[src for src, _ in jobs])


def _ffn_kernel(x_ref, g_ref, wi_ref, wo_ref, *refs):
    n_jobs = (len(refs) - 2) // 2
    o_ref, act_ref = refs[n_jobs], refs[-1]
    _run_cast_jobs(refs[:n_jobs], refs[n_jobs + 1:2 * n_jobs + 1])
    d_ff = wo_ref.shape[0]
    h = _rms(x_ref[...], g_ref[...]).astype(BF16)
    lo = 0
    for ck in FFN_CHUNKS:
        gate = jnp.dot(h, wi_ref[:, lo:lo + ck], preferred_element_type=F32)
        up = jnp.dot(h, wi_ref[:, d_ff + lo:d_ff + lo + ck], preferred_element_type=F32)
        act_ref[:, lo:lo + ck] = (_silu(gate) * up).astype(BF16)
        lo += ck
    y = jnp.dot(act_ref[...], wo_ref[...], preferred_element_type=F32)
    o_ref[...] = x_ref[...] + FFN_RESIDUAL * y


def _ffn(x2, gain, wi, wo, cast_jobs=()):
    n, d = x2.shape
    d_ff = wo.shape[0]
    assert sum(FFN_CHUNKS) == d_ff and n % TM_FFN == 0
    n_steps = n // TM_FFN
    job_in, job_out, job_shapes = _cast_job_specs(cast_jobs, n_steps)
    outs = pl.pallas_call(
        _ffn_kernel,
        grid=(n_steps,),
        in_specs=[
            pl.BlockSpec((TM_FFN, d), lambda i: (i, 0)),
            _resident((1, d)),
            _resident((d, 2 * d_ff)),
            _resident((d_ff, d)),
        ] + job_in,
        out_specs=[pl.BlockSpec((TM_FFN, d), lambda i: (i, 0))] + job_out,
        out_shape=[jax.ShapeDtypeStruct((n, d), F32)] + job_shapes,
        scratch_shapes=[pltpu.VMEM((TM_FFN, d_ff), BF16)],
        compiler_params=_params(1),
        name="ffn",
    )(x2, gain.reshape(1, d), wi, wo, *[src for src, _ in cast_jobs])
    return outs[0], outs[1:]


def _xattn_kernel(x_ref, g_ref, wq_ref, qn_ref, kt_ref, v_ref, wo_ref, o_ref, att_ref):
    d = x_ref.shape[-1]
    hd = d // MEM_HEADS
    h = _rms(x_ref[0], g_ref[...]).astype(BF16)
    q = jnp.dot(h, wq_ref[...], preferred_element_type=F32)
    for i in range(MEM_HEADS):
        cols = slice(i * hd, (i + 1) * hd)
        qh = _rms(q[:, cols], qn_ref[...]).astype(BF16)
        s = jnp.dot(qh, kt_ref[0, i], preferred_element_type=F32)
        p = jnp.exp2(s - jnp.max(s, axis=-1, keepdims=True))
        inv = 1.0 / jnp.sum(p, axis=-1, keepdims=True)
        o = jnp.dot(p.astype(BF16), v_ref[0, :, cols], preferred_element_type=F32)
        att_ref[:, cols] = (o * inv).astype(BF16)
    y = jnp.dot(att_ref[...], wo_ref[...], preferred_element_type=F32)
    o_ref[0] = x_ref[0] + y


def _xattn(x3, gain, wq, qnorm, mem_kt, mem_v, wo):
    b, t, d = x3.shape
    hd = d // MEM_HEADS
    m = mem_v.shape[1]
    tm = TM_XATTN
    assert t % tm == 0
    return pl.pallas_call(
        _xattn_kernel,
        grid=(b, t // tm),
        in_specs=[
            pl.BlockSpec((1, tm, d), lambda i, j: (i, j, 0)),
            _resident((1, d)),
            _resident((d, d)),
            _resident((1, hd)),
            pl.BlockSpec((1, MEM_HEADS, hd, m), lambda i, j: (i, 0, 0, 0)),
            pl.BlockSpec((1, m, d), lambda i, j: (i, 0, 0)),
            _resident((d, d)),
        ],
        out_specs=pl.BlockSpec((1, tm, d), lambda i, j: (i, j, 0)),
        out_shape=jax.ShapeDtypeStruct((b, t, d), F32),
        scratch_shapes=[pltpu.VMEM((tm, d), BF16)],
        compiler_params=_params(2),
        name="xattn",
    )(x3, gain.reshape(1, d), wq, (qnorm * (hd ** -0.5 * LOG2_E)).reshape(1, hd), mem_kt, mem_v, wo)


CONV_TILES = CONV_CH // LANES
POOL_TILES = D_POOL // LANES
TILES_PER_POOL_GROUP = POOL_GROUP // LANES


def _even_kernel(x_ref, g_ref, wzx_ref, wpool_ref, wdt_ref, cw_ref, cb_ref, dtb_ref, alog_ref,
                 dskip_ref, norm_ref,
                 pw_ref, ps_ref, wout_ref, exph_ref, tril_ref, o_ref,
                 conv_scr, act_scr, dt_scr, z_scr, pool_scr, state_scr, conv_halo, pool_halo):
    t = pl.program_id(1)
    tm = x_ref.shape[1]
    n_chunks = tm // CHUNK

    slot = lax.rem(t, 2)

    @pl.when(t == 0)
    def _():
        conv_halo[0] = jnp.zeros((CONV_TILES, CONV_HALO, LANES), F32)
        pool_halo[0] = jnp.zeros((POOL_TILES, POOL_HALO, LANES), F32)
        state_scr[...] = jnp.zeros(state_scr.shape, F32)

    conv_scr[:, 0:CONV_HALO, :] = conv_halo[slot]
    pool_scr[:, 0:POOL_HALO, :] = pool_halo[slot]

    x = x_ref[0]
    h = _rms(x, g_ref[...]).astype(BF16)
    xbc = jnp.dot(h, wzx_ref[:, D_SSM:], preferred_element_type=F32)
    for c in range(CONV_TILES):
        conv_scr[c, CONV_HALO:CONV_HALO + tm, :] = xbc[:, c * LANES:(c + 1) * LANES]
    u_pool = jnp.dot(h, wpool_ref[...], preferred_element_type=F32)
    for c in range(POOL_TILES):
        pool_scr[c, POOL_HALO:POOL_HALO + tm, :] = u_pool[:, c * LANES:(c + 1) * LANES]
    dt_raw = jnp.dot(h, wdt_ref[...], preferred_element_type=F32) + dtb_ref[...]
    dt_scr[...] = jnp.maximum(dt_raw, 0.0) + jnp.log1p(jnp.exp(-jnp.abs(dt_raw)))
    z_scr[...] = jnp.dot(h, wzx_ref[:, 0:D_SSM], preferred_element_type=F32)

    for c in range(CONV_TILES):
        cols = slice(c * LANES, (c + 1) * LANES)
        acc = cb_ref[:, cols]
        for k in range(CONV_WIDTH):
            acc = acc + (conv_scr[c, pl.ds(CONV_HALO - CONV_WIDTH + 1 + k, tm), :]
                         * cw_ref[k:k + 1, cols])
        act_scr[:, cols] = _silu(acc)
    conv_halo[1 - slot] = conv_scr[:, tm:tm + CONV_HALO, :]

    pos = t * tm + lax.broadcasted_iota(jnp.int32, (tm, 1), 0)
    y_pool = []
    for k, w in enumerate(POOL_WINDOWS):
        inv_count = 1.0 / jnp.minimum(pos + 1, w).astype(F32)
        pooled = []
        for c in range(k * TILES_PER_POOL_GROUP, (k + 1) * TILES_PER_POOL_GROUP):
            if w > SUBLANES:
                e = pool_scr[c, pl.ds(POOL_HALO - SUBLANES, tm + SUBLANES), :]
                for i in range(1, SUBLANES):
                    e = e + pool_scr[c, pl.ds(POOL_HALO - SUBLANES - i, tm + SUBLANES), :]
                s = e[SUBLANES:, :] + e[:tm, :]
            else:
                s = pool_scr[c, pl.ds(POOL_HALO, tm), :]
                for i in range(1, w):
                    s = s + pool_scr[c, pl.ds(POOL_HALO - i, tm), :]
            pooled.append(s * inv_count - pool_scr[c, pl.ds(POOL_HALO, tm), :])
        pooled = jnp.concatenate(pooled, axis=1).astype(BF16)
        kcols = slice(k * POOL_GROUP, (k + 1) * POOL_GROUP)
        yk = jnp.dot(pooled, pw_ref[k], preferred_element_type=F32)
        y_pool.append((yk * ps_ref[:, kcols]).astype(BF16))
    pool_halo[1 - slot] = pool_scr[:, tm:tm + POOL_HALO, :]

    a_row = -jnp.exp(alog_ref[...])
    low_half = (lax.broadcasted_iota(jnp.int32, (CHUNK, D_SSM), 1) & HALF) == 0
    causal = (lax.broadcasted_iota(jnp.int32, (CHUNK, CHUNK), 0)
              >= lax.broadcasted_iota(jnp.int32, (CHUNK, CHUNK), 1))
    heads_per_group = SSD_HEADS // SSD_GROUPS

    def chunk_body(c):
        rows = pl.ds(c * CHUNK, CHUNK)
        xs = act_scr[rows, 0:D_SSM]
        dt = dt_scr[rows, :]
        adt = dt * a_row
        hi = adt.astype(BF16)
        r1 = adt - hi.astype(F32)
        mid = r1.astype(BF16)
        lo = (r1 - mid.astype(F32)).astype(BF16)
        acs = jnp.dot(tril_ref[...], jnp.concatenate([hi, mid, lo], axis=0),
                      preferred_element_type=F32)
        acs_last = acs[CHUNK - 1:CHUNK, :]
        stacked = jnp.concatenate([dt, jnp.exp(acs_last - acs), jnp.exp(acs)], axis=0)
        s_hi, s_lo = _split2(stacked)
        expanded = jnp.dot(jnp.concatenate([s_hi, s_lo], axis=1), exph_ref[...],
                           preferred_element_type=F32)
        dt_exp = expanded[0:CHUNK]
        dte_exp = expanded[CHUNK:2 * CHUNK]
        dfs_exp = expanded[2 * CHUNK:3 * CHUNK]
        xdt = xs * dt_exp
        xw_b = (xdt * dte_exp).astype(BF16)
        x_even = jnp.where(low_half, xdt, 0.0).astype(BF16)
        x_odd = jnp.where(low_half, 0.0, xdt).astype(BF16)
        acs_row = acs.T

        y_tiles = []
        for g in range(SSD_GROUPS):
            gcols = slice(g * GROUP_W, (g + 1) * GROUP_W)
            b_g = act_scr[rows, D_SSM + g * D_STATE:D_SSM + (g + 1) * D_STATE]
            c_g = act_scr[rows, D_SSM + SSD_GROUPS * D_STATE + g * D_STATE:
                          D_SSM + SSD_GROUPS * D_STATE + (g + 1) * D_STATE].astype(BF16)
            bt_g = b_g.T.astype(BF16)
            cb = jnp.dot(c_g, bt_g, preferred_element_type=F32)
            s_new = jnp.dot(bt_g, xw_b[:, gcols], preferred_element_type=F32)
            s_in = state_scr[g]
            y_off = jnp.dot(c_g, s_in.astype(BF16), preferred_element_type=F32) * dfs_exp[:, gcols]
            state_scr[g] = s_in * dfs_exp[CHUNK - 1:CHUNK, gcols] + s_new
            for pair in range(heads_per_group // 2):
                pcols = slice(g * GROUP_W + pair * LANES, g * GROUP_W + (pair + 1) * LANES)
                w2 = []
                for parity in range(2):
                    hh = g * heads_per_group + 2 * pair + parity
                    seg = jnp.where(causal, jnp.exp(acs[:, hh:hh + 1] - acs_row[hh:hh + 1, :]), 0.0)
                    w2.append((cb * seg).astype(BF16))
                y_diag = jnp.dot(jnp.concatenate(w2, axis=1),
                                 jnp.concatenate([x_even[:, pcols], x_odd[:, pcols]], axis=0),
                                 preferred_element_type=F32)
                y_tiles.append(y_diag + y_off[:, pair * LANES:(pair + 1) * LANES])
        y = (jnp.concatenate(y_tiles, axis=1) + dskip_ref[...] * xs) * _silu(z_scr[rows, :])
        return jnp.concatenate(
            [_rms(y[:, g * GROUP_W:(g + 1) * GROUP_W], norm_ref[:, g * GROUP_W:(g + 1) * GROUP_W])
             for g in range(SSD_GROUPS)], axis=1).astype(BF16)

    y_ssd = jnp.concatenate([chunk_body(c) for c in range(n_chunks)], axis=0)
    o_ref[0] = x + jnp.dot(jnp.concatenate([y_ssd] + y_pool, axis=1), wout_ref[...],
                           preferred_element_type=F32)


def _head_expand_matrix():
    m = np.zeros((2 * LANES, D_SSM), np.float32)
    for hh in range(SSD_HEADS):
        m[hh, hh * SSD_HEAD_DIM:(hh + 1) * SSD_HEAD_DIM] = 1.0
        m[LANES + hh, hh * SSD_HEAD_DIM:(hh + 1) * SSD_HEAD_DIM] = 1.0
    return jnp.asarray(m, BF16)


def _even_mixer(x3, gain, in_proj, conv_w, conv_b, dt_bias, a_log, d_skip, ssd_norm,
                pool_w, pool_scale, out_proj):
    b, t, d = x3.shape
    tm = TM_EVEN
    assert t % tm == 0 and tm % CHUNK == 0
    o_dt = D_SSM + CONV_CH
    o_pool = o_dt + SSD_HEADS
    pad = LANES - SSD_HEADS
    w_bf = in_proj.astype(BF16)
    w_zx, w_pool = w_bf[:, :o_dt], w_bf[:, o_pool:]
    w_dt = jnp.pad(w_bf[:, o_dt:o_pool], ((0, 0), (0, pad)))
    dtb = jnp.pad(dt_bias, (0, pad)).reshape(1, LANES)
    alog = jnp.pad(a_log, (0, pad)).reshape(1, LANES)
    dskip = jnp.repeat(d_skip, SSD_HEAD_DIM).reshape(1, D_SSM)
    tril3 = jnp.asarray(np.tile(np.tril(np.ones((CHUNK, CHUNK), np.float32)), (1, 3)), BF16)
    n_g = len(POOL_WINDOWS)
    return pl.pallas_call(
        _even_kernel,
        grid=(b, t // tm),
        in_specs=[
            pl.BlockSpec((1, tm, d), lambda i, j: (i, j, 0)),
            _resident((1, d)),
            _resident((d, o_dt)),
            _resident((d, D_POOL)),
            _resident((d, LANES)),
            _resident((CONV_WIDTH, CONV_CH)),
            _resident((1, CONV_CH)),
            _resident((1, LANES)),
            _resident((1, LANES)),
            _resident((1, D_SSM)),
            _resident((1, D_SSM)),
            _resident((n_g, POOL_GROUP, POOL_GROUP)),
            _resident((1, D_POOL)),
            _resident((D_SSM + D_POOL, d)),
            _resident((2 * LANES, D_SSM)),
            _resident((CHUNK, 3 * CHUNK)),
        ],
        out_specs=pl.BlockSpec((1, tm, d), lambda i, j: (i, j, 0)),
        out_shape=jax.ShapeDtypeStruct((b, t, d), F32),
        scratch_shapes=[
            pltpu.VMEM((CONV_TILES, tm + CONV_HALO, LANES), F32),
            pltpu.VMEM((tm, CONV_CH), F32),
            pltpu.VMEM((tm, LANES), F32),
            pltpu.VMEM((tm, D_SSM), F32),
            pltpu.VMEM((POOL_TILES, tm + POOL_HALO, LANES), F32),
            pltpu.VMEM((SSD_GROUPS, D_STATE, GROUP_W), F32),
            pltpu.VMEM((2, CONV_TILES, CONV_HALO, LANES), F32),
            pltpu.VMEM((2, POOL_TILES, POOL_HALO, LANES), F32),
        ],
        compiler_params=_params(2),
        name="even_mixer",
    )(x3, gain.reshape(1, d), w_zx, w_pool, w_dt, conv_w, conv_b.reshape(1, CONV_CH), dtb, alog,
      dskip,
      ssd_norm.reshape(1, D_SSM), pool_w.reshape(n_g, POOL_GROUP, POOL_GROUP),
      pool_scale.reshape(1, D_POOL), out_proj, _head_expand_matrix(), tril3)


N_QK_HEADS = ATTN_HEADS + ATTN_KV_HEADS
QK_W = N_QK_HEADS * ATTN_HEAD_DIM
Q_W = ATTN_HEADS * ATTN_HEAD_DIM
KV_W = ATTN_KV_HEADS * ATTN_HEAD_DIM
KV_PAIRS = ATTN_KV_HEADS // 2
STACK = ATTN_GROUP * BLOCK
WO_BLOCKS = 4


def _odd_kernel(sink_ref, x_ref, g_ref, wqkv_ref, bqkv_ref, gqk_ref, headmean_ref, bias_ref,
                wo_ref, bo_ref, o_ref, q_scr, k_scr, v_scr, k_prev, v_prev):
    t = pl.program_id(1)
    tm = x_ref.shape[1]
    n_blocks = tm // BLOCK
    n_forms = 2 * KV_PAIRS

    slot = lax.rem(t, 2)

    @pl.when(t == 0)
    def _():
        k_prev[0] = jnp.zeros((n_forms, BLOCK, LANES), BF16)
        v_prev[0] = jnp.zeros((n_forms, BLOCK, 2 * LANES), BF16)

    k_scr[:, 0:BLOCK, :] = k_prev[slot]
    v_scr[:, 0:BLOCK, :] = v_prev[slot]

    x = x_ref[0]
    h = _rms(x, g_ref[...]).astype(BF16)
    qkv = jnp.dot(h, wqkv_ref[...], preferred_element_type=F32) + bqkv_ref[...]
    qk = qkv[:, 0:QK_W]
    sq = (qk * qk).astype(BF16)
    msq = jnp.concatenate(
        [jnp.dot(sq[:, c0:c0 + MXU_WIDTH], headmean_ref[...], preferred_element_type=F32)
         for c0 in range(0, QK_W, MXU_WIDTH)], axis=1)
    qkn = qk * lax.rsqrt(msq + EPS) * gqk_ref[...]
    q_scr[...] = qkn[:, 0:Q_W].astype(BF16)

    low_half = lax.broadcasted_iota(jnp.int32, (tm, LANES), 1) < HALF
    zeros = jnp.zeros((tm, LANES), F32)
    ones = (jnp.where(low_half, 1.0, 0.0).astype(BF16), jnp.where(low_half, 0.0, 1.0).astype(BF16))
    for is_v, src, dst, base in ((False, qkn, k_scr, Q_W), (True, qkv, v_scr, QK_W)):
        for kp in range(KV_PAIRS):
            a2 = src[:, base + kp * LANES:base + (kp + 1) * LANES]
            for f, val in enumerate((jnp.where(low_half, a2, zeros), jnp.where(low_half, zeros, a2))):
                dst[2 * kp + f, BLOCK:BLOCK + tm, 0:LANES] = val.astype(BF16)
                if is_v:
                    dst[2 * kp + f, BLOCK:BLOCK + tm, LANES:2 * LANES] = ones[f]

    nt_dims = (((1,), (1,)), ((), ()))
    row = lax.broadcasted_iota(jnp.int32, (STACK, LANES), 0)
    col = lax.broadcasted_iota(jnp.int32, (STACK, LANES), 1)
    tri = col <= (row & (BLOCK - 1))
    lane_low = col < HALF
    row_group = lax.broadcasted_iota(jnp.int32, (STACK, 1), 0) // BLOCK

    def sink_column(kvh):
        sink = jnp.full((STACK, 1), sink_ref[kvh * ATTN_GROUP], F32)
        for g in range(1, ATTN_GROUP):
            sink = jnp.where(row_group == g, sink_ref[kvh * ATTN_GROUP + g], sink)
        return sink

    def block_attention(i):
        rows = pl.ds(i * BLOCK, BLOCK)
        krows = pl.ds(i * BLOCK, 2 * BLOCK)
        sel = jnp.where(t == 0, 1, 0) if i == 0 else 0
        att_tiles = []
        for kp in range(KV_PAIRS):
            col0 = kp * ATTN_GROUP * LANES
            q4 = jnp.concatenate(
                [q_scr[rows, col0 + g * LANES:col0 + (g + 1) * LANES] for g in range(ATTN_GROUP)],
                axis=0)
            acc = None
            shifts = []
            for f in range(2):
                s2 = lax.dot_general(q4, k_scr[2 * kp + f, krows, :], nt_dims,
                                     preferred_element_type=F32)
                sc = jnp.where(tri, s2[:, BLOCK:], s2[:, :BLOCK]) + bias_ref[sel, kp, f]
                sink = sink_column(2 * kp + f)
                m = jnp.maximum(jnp.max(sc, axis=-1, keepdims=True), sink)
                p = jnp.exp(sc - m)
                pcat = jnp.concatenate([jnp.where(tri, 0.0, p), jnp.where(tri, p, 0.0)],
                                       axis=1).astype(BF16)
                part = jnp.dot(pcat, v_scr[2 * kp + f, krows, :],
                               preferred_element_type=F32)
                acc = part if acc is None else acc + part
                shifts.append(sink - m)
            denom = acc[:, LANES:] + jnp.exp(jnp.where(lane_low, shifts[0], shifts[1]))
            att = (acc[:, :LANES] * (1.0 / denom)).astype(BF16)
            att_tiles.extend(att[g * BLOCK:(g + 1) * BLOCK] for g in range(ATTN_GROUP))
        return jnp.concatenate(att_tiles, axis=1)

    for i0 in range(0, n_blocks, WO_BLOCKS):
        att = jnp.concatenate([block_attention(i) for i in range(i0, i0 + WO_BLOCKS)], axis=0)
        rows = pl.ds(i0 * BLOCK, WO_BLOCKS * BLOCK)
        o_ref[0, rows, :] = (x_ref[0, rows, :] + bo_ref[...]
                             + jnp.dot(att, wo_ref[...], preferred_element_type=F32))
    k_prev[1 - slot] = k_scr[:, tm:tm + BLOCK, :]
    v_prev[1 - slot] = v_scr[:, tm:tm + BLOCK, :]


def _alibi_bias():
    slopes = 2.0 ** (-8.0 * (np.arange(ATTN_HEADS) + 1) / ATTN_HEADS)
    q = np.arange(BLOCK)[:, None]
    j = np.arange(BLOCK)[None, :]
    cur = j <= q
    dist = np.where(cur, q - j, q - j + BLOCK).astype(np.float32)
    assert dist.min() >= 0 and dist.max() < WINDOW
    out = np.zeros((2, KV_PAIRS, 2, STACK, BLOCK), np.float32)
    for kp in range(KV_PAIRS):
        for f in range(2):
            for g in range(ATTN_GROUP):
                head = (2 * kp + f) * ATTN_GROUP + g
                base = -np.float32(slopes[head]) * dist
                out[0, kp, f, g * BLOCK:(g + 1) * BLOCK] = base
                out[1, kp, f, g * BLOCK:(g + 1) * BLOCK] = np.where(cur, base, -np.inf)
    return jnp.asarray(out)


def _to_kernel_head_order(a, axis):
    shape = a.shape
    split = shape[:axis] + (KV_PAIRS, 2, ATTN_GROUP, ATTN_HEAD_DIM) + shape[axis + 1:]
    return jnp.swapaxes(a.reshape(split), axis + 1, axis + 2).reshape(shape)


def _head_mean_matrix():
    idx = np.arange(MXU_WIDTH) // ATTN_HEAD_DIM
    return jnp.asarray((idx[:, None] == idx[None, :]).astype(np.float32) / ATTN_HEAD_DIM, BF16)


def _odd_mixer(x3, gain, wqkv, bqkv, qnorm, knorm, sinks, wo, bo):
    b, t, d = x3.shape
    tm = TM_ODD
    assert t % tm == 0 and tm % (WO_BLOCKS * BLOCK) == 0 and QK_W % MXU_WIDTH == 0
    qkv_w = wqkv.shape[1]
    gqk = jnp.concatenate([jnp.tile(qnorm, ATTN_HEADS) * (ATTN_HEAD_DIM ** -0.5),
                           jnp.tile(knorm, ATTN_KV_HEADS)]).reshape(1, QK_W)
    wqkv = jnp.concatenate([_to_kernel_head_order(wqkv[:, :Q_W], 1), wqkv[:, Q_W:]], axis=1)
    bqkv = jnp.concatenate([_to_kernel_head_order(bqkv[:Q_W], 0), bqkv[Q_W:]])
    wo = _to_kernel_head_order(wo, 0)
    n_forms = 2 * KV_PAIRS
    return pl.pallas_call(
        _odd_kernel,
        grid=(b, t // tm),
        in_specs=[
            pl.BlockSpec(memory_space=pltpu.SMEM),
            pl.BlockSpec((1, tm, d), lambda i, j: (i, j, 0)),
            _resident((1, d)),
            _resident((d, qkv_w)),
            _resident((1, qkv_w)),
            _resident((1, QK_W)),
            _resident((MXU_WIDTH, MXU_WIDTH)),
            _resident((2, KV_PAIRS, 2, STACK, BLOCK)),
            _resident((Q_W, d)),
            _resident((1, d)),
        ],
        out_specs=pl.BlockSpec((1, tm, d), lambda i, j: (i, j, 0)),
        out_shape=jax.ShapeDtypeStruct((b, t, d), F32),
        scratch_shapes=[
            pltpu.VMEM((tm, Q_W), BF16),
            pltpu.VMEM((n_forms, tm + BLOCK, LANES), BF16),
            pltpu.VMEM((n_forms, tm + BLOCK, 2 * LANES), BF16),
            pltpu.VMEM((2, n_forms, BLOCK, LANES), BF16),
            pltpu.VMEM((2, n_forms, BLOCK, 2 * LANES), BF16),
        ],
        compiler_params=_params(2),
        name="odd_mixer",
    )(sinks, x3, gain.reshape(1, d), wqkv, bqkv.reshape(1, qkv_w), gqk,
      _head_mean_matrix(), _alibi_bias(), wo, bo.reshape(1, d))


def kernel(x, mem, mem_norm, mem_wkv, mem_knorm, ffn1_norm, ffn1_wi, ffn1_wo, mix_norm, ssd_in_proj, ssd_conv_w, ssd_conv_b, ssd_dt_bias, ssd_a_log, ssd_d, ssd_norm, pool_w, pool_scale, even_out_proj, attn_wqkv, attn_bqkv, attn_qnorm, attn_knorm, attn_sinks, attn_wo, attn_bo, xattn_norm, xattn_wq, xattn_qnorm, xattn_wo, ffn2_norm, ffn2_wi, ffn2_wo):
    b, t, d = x.shape
    depth = ffn1_norm.shape[0]
    mem_kt, mem_v = _mem_kv(mem, mem_norm, mem_wkv, mem_knorm)
    pool_w2 = pool_w.reshape(pool_w.shape[0], D_POOL, POOL_GROUP)
    wi1, wo1 = _cast_weights([(ffn1_wi, 0), (ffn1_wo, 0)])
    for i in range(depth):
        jobs = [(ffn2_wi, i), (ffn2_wo, i), (xattn_wq, i), (xattn_wo, i)]
        if i % 2 == 0:
            jobs += [(pool_w2, i // 2), (even_out_proj, i // 2)]
        else:
            jobs += [(attn_wqkv, i // 2), (attn_wo, i // 2)]
        x2, (wi2, wo2, xq, xo, mix_a, mix_b) = _ffn(x.reshape(b * t, d), ffn1_norm[i], wi1, wo1, jobs)
        x = x2.reshape(b, t, d)
        if i % 2 == 0:
            e = i // 2
            x = _even_mixer(x, mix_norm[i], ssd_in_proj[e], ssd_conv_w[e], ssd_conv_b[e],
                            ssd_dt_bias[e], ssd_a_log[e], ssd_d[e], ssd_norm[e], mix_a,
                            pool_scale[e], mix_b)
        else:
            o = i // 2
            x = _odd_mixer(x, mix_norm[i], mix_a, attn_bqkv[o], attn_qnorm[o], attn_knorm[o],
                           attn_sinks[o], mix_b, attn_bo[o])
        x = _xattn(x, xattn_norm[i], xq, xattn_qnorm[i], mem_kt, mem_v, xo)
        jobs = [(ffn1_wi, i + 1), (ffn1_wo, i + 1)] if i + 1 < depth else []
        x2, nxt = _ffn(x.reshape(b * t, d), ffn2_norm[i], wi2, wo2, jobs)
        x = x2.reshape(b, t, d)
        if nxt:
            wi1, wo1 = nxt
    return x
```
